```python
import math
import jax, jax.numpy as jnp
from jax import lax
import numpy as np

D_MODEL = 2048
BATCH = 4
SEQ = 4096
DEPTH = 4

CHUNK = 64
EPS = 1e-6

S5_GROUPS = 32
S5_GROUP_DIM = 16
S5_WIDTH = S5_GROUPS * S5_GROUP_DIM
S5_STATE = 64
S5_DT_MIN = 1e-3
S5_DT_MAX = 1e-1

HG_HEADS = 6
HG_DK = 128
HG_DV = 128
HG_KEY_WIDTH = HG_HEADS * HG_DK
HG_WIDTH = HG_HEADS * HG_DV

ML_HEADS = 4
ML_DH = 192
ML_WIDTH = ML_HEADS * ML_DH
ML_CONV = 4

MIX_WIDTH = S5_WIDTH + HG_WIDTH + ML_WIDTH
N_BRANCH = 3

FFN_DIM = 5632
FFN_CONV = 3

IN_WIDTHS = (S5_WIDTH,
             HG_KEY_WIDTH, HG_KEY_WIDTH,
             HG_WIDTH, HG_WIDTH,
             ML_WIDTH, ML_WIDTH, ML_WIDTH,
             ML_HEADS, ML_HEADS,
             N_BRANCH * D_MODEL)
IN_TOTAL = 12040

kernel_name = "hybrid_s5_hgrn2_mlstm_convffn"


def _split_points():
    pts, acc = [], 0
    for w in IN_WIDTHS[:-1]:
        acc += w
        pts.append(acc)
    return pts


def rms_norm(x, g):
    xf = x.astype(jnp.float32)
    y = xf * lax.rsqrt(jnp.mean(xf * xf, axis=-1, keepdims=True) + EPS)
    return (y * g.astype(jnp.float32)).astype(x.dtype)


def causal_dwconv(x, w, b):
    k, c = w.shape[0], x.shape[-1]
    y = lax.conv_general_dilated(x, w[:, None, :].astype(x.dtype), window_strides=(1,),
                                 padding=((k - 1, 0),), dimension_numbers=('NWC', 'WIO', 'NWC'),
                                 feature_group_count=c)
    return y + b.astype(x.dtype)


def to_chunks(t, n_heads, d):
    bsz, seq = t.shape[0], t.shape[1]
    return t.reshape(bsz, seq // CHUNK, CHUNK, n_heads, d).transpose(1, 0, 3, 2, 4)


def from_chunks(t):
    nc, bsz, h, l, d = t.shape
    return t.transpose(1, 0, 3, 2, 4).reshape(bsz, nc * l, h, d)


def s5_mixer(u, lam_re, lam_im, log_dt, b_re, b_im, c_re, c_im, d_skip, w_glu, b_glu):
    bsz, seq, _ = u.shape
    f32 = jnp.float32
    uf = u.astype(f32).reshape(bsz, seq, S5_GROUPS, S5_GROUP_DIM)
    dt = jnp.exp(log_dt.astype(f32))[:, None]
    lr, li = lam_re.astype(f32), lam_im.astype(f32)
    mag = jnp.exp(lr * dt)
    ar, ai = mag * jnp.cos(li * dt), mag * jnp.sin(li * dt)
    den = lr * lr + li * li
    cr = ((ar - 1.0) * lr + ai * li) / den
    ci = (ai * lr - (ar - 1.0) * li) / den
    br_, bi_ = b_re.astype(f32), b_im.astype(f32)
    bbr = cr[..., None] * br_ - ci[..., None] * bi_
    bbi = cr[..., None] * bi_ + ci[..., None] * br_
    xr = jnp.einsum('bsgp,gnp->bsgn', uf, bbr)
    xi = jnp.einsum('bsgp,gnp->bsgn', uf, bbi)
    ar_f = jnp.broadcast_to(ar, xr.shape)
    ai_f = jnp.broadcast_to(ai, xr.shape)

    def combine(e1, e2):
        a1r, a1i, x1r, x1i = e1
        a2r, a2i, x2r, x2i = e2
        return (a2r * a1r - a2i * a1i, a2r * a1i + a2i * a1r,
                a2r * x1r - a2i * x1i + x2r, a2r * x1i + a2i * x1r + x2i)

    _, _, sr, si = lax.associative_scan(combine, (ar_f, ai_f, xr, xi), axis=1)
    y = (jnp.einsum('bsgn,gpn->bsgp', sr, c_re.astype(f32))
         - jnp.einsum('bsgn,gpn->bsgp', si, c_im.astype(f32))
         + d_skip.astype(f32).reshape(S5_GROUPS, S5_GROUP_DIM) * uf)
    y = jax.nn.gelu(y.reshape(bsz, seq, S5_WIDTH))
    y = y * jax.nn.sigmoid(y @ w_glu.astype(f32) + b_glu.astype(f32))
    return y.astype(u.dtype)


def hgrn2_mixer(q_in, f_in, i_in, g_in, lb, norm_g):
    bsz, seq, _ = q_in.shape
    f32 = jnp.float32
    q = jax.nn.silu(q_in.astype(f32))
    lbf = lb.astype(f32)
    f = lbf + (1.0 - lbf) * jax.nn.sigmoid(f_in.astype(f32))
    logf = jnp.log(f)
    k = 1.0 - f
    v = i_in.astype(f32)
    qc, kc = to_chunks(q, HG_HEADS, HG_DK), to_chunks(k, HG_HEADS, HG_DK)
    vc, gc = to_chunks(v, HG_HEADS, HG_DV), to_chunks(logf, HG_HEADS, HG_DK)
    causal = jnp.tril(jnp.ones((CHUNK, CHUNK), dtype=bool))

    def step(state, xs):
        qb, kb, vb, gb = xs
        cum = jnp.cumsum(gb, axis=2)
        last = cum[:, :, -1:, :]
        inter = jnp.einsum('bhtk,bhkv->bhtv', qb * jnp.exp(cum), state)
        diff = cum[:, :, :, None, :] - cum[:, :, None, :, :]
        decay = jnp.exp(jnp.where(causal[:, :, None], diff, -jnp.inf))
        scores = jnp.einsum('bhtk,bhsk,bhtsk->bhts', qb, kb, decay)
        out = inter + jnp.einsum('bhts,bhsv->bhtv', scores, vb)
        new_state = (jnp.exp(last[:, :, 0, :])[..., None] * state
                     + jnp.einsum('bhsk,bhsv->bhkv', kb * jnp.exp(last - cum), vb))
        return new_state, out

    s0 = jnp.zeros((bsz, HG_HEADS, HG_DK, HG_DV), f32)
    _, o = lax.scan(step, s0, (qc, kc, vc, gc))
    o = rms_norm(from_chunks(o), norm_g.reshape(HG_HEADS, HG_DV))
    o = o.reshape(bsz, seq, HG_WIDTH) * jax.nn.silu(g_in.astype(f32))
    return o.astype(q_in.dtype)


def mlstm_mixer(cx, v_in, o_in, ig_in, fg_in, conv_w, conv_b, w_qk, b_ig, b_fg, norm_g):
    bsz, seq, _ = cx.shape
    f32 = jnp.float32
    ca = jax.nn.silu(causal_dwconv(cx, conv_w, conv_b)).reshape(bsz, seq, ML_HEADS, ML_DH)
    qk = jnp.einsum('bshd,hde->bshe', ca, w_qk.astype(ca.dtype)).astype(f32)
    q = qk[..., :ML_DH].reshape(bsz, seq, ML_WIDTH)
    k = (qk[..., ML_DH:] * (ML_DH ** -0.5)).reshape(bsz, seq, ML_WIDTH)
    v = v_in.astype(f32)
    ig = (ig_in + b_ig).astype(f32)
    lf = jax.nn.log_sigmoid((fg_in + b_fg).astype(f32))
    qc, kc, vc = to_chunks(q, ML_HEADS, ML_DH), to_chunks(k, ML_HEADS, ML_DH), to_chunks(v, ML_HEADS, ML_DH)
    nch = seq // CHUNK
    icc = ig.reshape(bsz, nch, CHUNK, ML_HEADS).transpose(1, 0, 3, 2)
    fcc = lf.reshape(bsz, nch, CHUNK, ML_HEADS).transpose(1, 0, 3, 2)
    causal = jnp.tril(jnp.ones((CHUNK, CHUNK), dtype=bool))

    def step(carry, xs):
        cmat, nvec, m = carry
        qb, kb, vb, ib, fb = xs
        bcum = jnp.cumsum(fb, axis=-1)
        dmat = jnp.where(causal, bcum[..., :, None] - bcum[..., None, :] + ib[..., None, :], -jnp.inf)
        inter_log = bcum + m[..., None]
        m_t = jnp.maximum(inter_log, jnp.max(dmat, axis=-1))
        w_inter = jnp.exp(inter_log - m_t)
        w_intra = jnp.exp(dmat - m_t[..., None]) * jnp.einsum('bhtd,bhsd->bhts', qb, kb)
        num = (w_inter[..., None] * jnp.einsum('bhtk,bhkv->bhtv', qb, cmat)
               + jnp.einsum('bhts,bhsv->bhtv', w_intra, vb))
        den = w_inter * jnp.einsum('bhtk,bhk->bht', qb, nvec) + jnp.sum(w_intra, axis=-1)
        h = num / jnp.maximum(jnp.abs(den), jnp.exp(-m_t))[..., None]
        m_new = m_t[..., -1]
        decay = jnp.exp(bcum[..., -1] + m - m_new)
        ws = jnp.exp(bcum[..., -1:] - bcum + ib - m_new[..., None])
        c_new = decay[..., None, None] * cmat + jnp.einsum('bhs,bhsk,bhsv->bhkv', ws, kb, vb)
        n_new = decay[..., None] * nvec + jnp.einsum('bhs,bhsk->bhk', ws, kb)
        return (c_new, n_new, m_new), h

    init = (jnp.zeros((bsz, ML_HEADS, ML_DH, ML_DH), f32),
            jnp.zeros((bsz, ML_HEADS, ML_DH), f32),
            jnp.zeros((bsz, ML_HEADS), f32))
    _, h = lax.scan(step, init, (qc, kc, vc, icc, fcc))
    h = rms_norm(from_chunks(h), norm_g.reshape(ML_HEADS, ML_DH))
    h = h.reshape(bsz, seq, ML_WIDTH) * jax.nn.sigmoid(o_in.astype(f32))
    return h.astype(cx.dtype)


def conv_ffn(h, w_up, conv_w, conv_b, w_down):
    up = causal_dwconv(h @ w_up, conv_w, conv_b)
    a, b = up[..., :FFN_DIM], up[..., FFN_DIM:]
    return (jax.nn.silu(a) * b) @ w_down


def setup_inputs(seed: int = 0) -> dict:
    key = jax.random.key(seed)
    ks = jax.random.split(key, 32)
    f32 = jnp.float32
    L, G, N, P = DEPTH, S5_GROUPS, S5_STATE, S5_GROUP_DIM

    def nrm(k, shape, scale):
        return scale * jax.random.normal(k, shape, f32)

    n_idx = jnp.arange(N, dtype=f32)
    return {
        'x': nrm(ks[0], (BATCH, SEQ, D_MODEL), 1.0),
        'mix_norm': 1.0 + nrm(ks[1], (L, D_MODEL), 0.02),
        'w_in': nrm(ks[2], (L, D_MODEL, IN_TOTAL), D_MODEL ** -0.5),
        's5_lam_re': -0.5 + nrm(ks[3], (L, G, N), 0.01),
        's5_lam_im': math.pi * n_idx + nrm(ks[4], (L, G, N), 0.01),
        's5_log_dt': jax.random.uniform(ks[5], (L, G), f32, math.log(S5_DT_MIN), math.log(S5_DT_MAX)),
        's5_b_re': nrm(ks[6], (L, G, N, P), (2 * P) ** -0.5),
        's5_b_im': nrm(ks[7], (L, G, N, P), (2 * P) ** -0.5),
        's5_c_re': nrm(ks[8], (L, G, P, N), N ** -0.5),
        's5_c_im': nrm(ks[9], (L, G, P, N), N ** -0.5),
        's5_d': nrm(ks[10], (L, S5_WIDTH), 1.0),
        's5_w_glu': nrm(ks[11], (L, S5_WIDTH, S5_WIDTH), S5_WIDTH ** -0.5),
        's5_b_glu': nrm(ks[12], (L, S5_WIDTH), 0.02),
        'hg_lower_bounds': nrm(ks[13], (L, HG_KEY_WIDTH), 1.0),
        'hg_norm': 1.0 + nrm(ks[14], (L, HG_WIDTH), 0.02),
        'ml_conv_w': nrm(ks[15], (L, ML_CONV, ML_WIDTH), 0.5),
        'ml_conv_b': nrm(ks[16], (L, ML_WIDTH), 0.02),
        'ml_w_qk': nrm(ks[17], (L, ML_HEADS, ML_DH, 2 * ML_DH), ML_DH ** -0.5),
        'ml_b_ig': nrm(ks[18], (L, ML_HEADS), 0.1),
        'ml_b_fg': jnp.linspace(3.0, 6.0, ML_HEADS, dtype=f32)[None, :] + nrm(ks[19], (L, ML_HEADS), 0.01),
        'ml_norm': 1.0 + nrm(ks[20], (L, ML_WIDTH), 0.02),
        'w_branch': nrm(ks[21], (L, MIX_WIDTH, D_MODEL), (MIX_WIDTH / N_BRANCH) ** -0.5),
        'w_out': nrm(ks[22], (L, D_MODEL, D_MODEL), D_MODEL ** -0.5),
        'ffn_norm': 1.0 + nrm(ks[23], (L, D_MODEL), 0.02),
        'ffn_w_up': nrm(ks[24], (L, D_MODEL, 2 * FFN_DIM), D_MODEL ** -0.5),
        'ffn_conv_w': nrm(ks[25], (L, FFN_CONV, 2 * FFN_DIM), FFN_CONV ** -0.5),
        'ffn_conv_b': nrm(ks[26], (L, 2 * FFN_DIM), 0.02),
        'ffn_w_down': nrm(ks[27], (L, FFN_DIM, D_MODEL), FFN_DIM ** -0.5),
        'final_norm': 1.0 + nrm(ks[28], (D_MODEL,), 0.02),
    }


def reference(x, mix_norm, w_in, s5_lam_re, s5_lam_im, s5_log_dt, s5_b_re, s5_b_im, s5_c_re, s5_c_im,
              s5_d, s5_w_glu, s5_b_glu, hg_lower_bounds, hg_norm, ml_conv_w, ml_conv_b, ml_w_qk,
              ml_b_ig, ml_b_fg, ml_norm, w_branch, w_out, ffn_norm, ffn_w_up, ffn_conv_w, ffn_conv_b,
              ffn_w_down, final_norm):
    lbs = jax.nn.softmax(hg_lower_bounds.astype(jnp.float32), axis=0)
    lbs = jnp.cumsum(lbs, axis=0) - lbs[0:1]
    splits = _split_points()
    r_a, r_b = S5_WIDTH, S5_WIDTH + HG_WIDTH
    for l in range(DEPTH):
        h = rms_norm(x, mix_norm[l])
        (u_a, q_b, f_b, i_b, og_b, cx_c, v_c, o_c, ig_c, fg_c, gates) = jnp.split(h @ w_in[l], splits, axis=-1)
        y_a = s5_mixer(u_a, s5_lam_re[l], s5_lam_im[l], s5_log_dt[l], s5_b_re[l], s5_b_im[l],
                       s5_c_re[l], s5_c_im[l], s5_d[l], s5_w_glu[l], s5_b_glu[l])
        y_b = hgrn2_mixer(q_b, f_b, i_b, og_b, lbs[l], hg_norm[l])
        y_c = mlstm_mixer(cx_c, v_c, o_c, ig_c, fg_c, ml_conv_w[l], ml_conv_b[l], ml_w_qk[l],
                          ml_b_ig[l], ml_b_fg[l], ml_norm[l])
        wb = w_branch[l]
        g_a, g_b, g_c = (gates[..., :D_MODEL], gates[..., D_MODEL:2 * D_MODEL], gates[..., 2 * D_MODEL:])
        merged = (jax.nn.sigmoid(g_a) * (y_a @ wb[:r_a])
                  + jax.nn.sigmoid(g_b) * (y_b @ wb[r_a:r_b])
                  + jax.nn.sigmoid(g_c) * (y_c @ wb[r_b:]))
        x = x + merged @ w_out[l]
        x = x + conv_ffn(rms_norm(x, ffn_norm[l]), ffn_w_up[l], ffn_conv_w[l], ffn_conv_b[l], ffn_w_down[l])
    return rms_norm(x, final_norm)
```

```python
import functools
import math

import jax
import jax.numpy as jnp
from jax import lax
from jax.experimental import pallas as pl
from jax.experimental.pallas import tpu as pltpu

f32 = jnp.float32
bf16 = jnp.bfloat16

D_MODEL = 2048
EPS = 1e-6
CHUNK = 64

S5_GROUPS = 32
S5_P = 16
S5_WIDTH = 512
S5_STATE = 64
S5_L = 16
S5_COLS = S5_L * S5_P

HG_HEADS = 6
HG_D = 128
HG_WIDTH = 768
HG_SUB = 16

ML_HEADS = 4
ML_DH = 192
ML_DP = 256
ML_WIDTH = 768
ML_WP = ML_HEADS * ML_DP
ML_CONV = 4

FFN_DIM = 5632
FFN_CONV = 3
FFN_TN = 512
FFN_NJ = FFN_DIM // FFN_TN

OFF_U, OFF_Q, OFF_F, OFF_I, OFF_OG = 0, 512, 1280, 2048, 2816
OFF_CX, OFF_V, OFF_O, OFF_IGFG = 3584, 4608, 5632, 6656
MIXER_COLS = 7168
GATE_COLS = 3 * D_MODEL

NEG = -1e30


def _split3(x):
    hi = x.astype(bf16)
    r1 = x - hi.astype(f32)
    mid = r1.astype(bf16)
    lo = (r1 - mid.astype(f32)).astype(bf16)
    return hi, mid, lo


def _dot(a, b):
    return jnp.dot(a, b, preferred_element_type=f32)


def _dot_nt(a, b):
    return lax.dot_general(a, b, (((1,), (1,)), ((), ())), preferred_element_type=f32)


def _dot_tn(a, b):
    return lax.dot_general(a, b, (((0,), (0,)), ((), ())), preferred_element_type=f32)


def _exact_dot(a_bf16, x):
    hi, mid, lo = _split3(x)
    return _dot(a_bf16, hi) + _dot(a_bf16, mid) + _dot(a_bf16, lo)


def _rms(x, g):
    ms = jnp.mean(x * x, axis=-1, keepdims=True)
    return x * lax.rsqrt(ms + EPS) * g


def _norm_proj_kernel(x_ref, g_ref, w_ref, o_ref, hs_ref, *, act):
    @pl.when(pl.program_id(1) == 0)
    def _():
        hs_ref[...] = _rms(x_ref[...], g_ref[...]).astype(bf16)

    y = _dot(hs_ref[...], w_ref[...])
    if act == "sigmoid":
        y = jax.nn.sigmoid(y)
    o_ref[...] = y.astype(o_ref.dtype)


def _norm_proj(x, g, w, *, act, out_dtype, tm, tn):
    t, d = x.shape
    n = w.shape[1]
    return pl.pallas_call(
        functools.partial(_norm_proj_kernel, act=act),
        grid=(t // tm, n // tn),
        in_specs=[
            pl.BlockSpec((tm, d), lambda i, j: (i, 0)),
            pl.BlockSpec((1, d), lambda i, j: (0, 0)),
            pl.BlockSpec((d, tn), lambda i, j: (0, j)),
        ],
        out_specs=pl.BlockSpec((tm, tn), lambda i, j: (i, j)),
        out_shape=jax.ShapeDtypeStruct((t, n), out_dtype),
        scratch_shapes=[pltpu.VMEM((tm, d), bf16)],
        compiler_params=pltpu.CompilerParams(dimension_semantics=("parallel", "arbitrary")),
        name="norm_proj_" + act,
    )(x, g, w)


def _s5_kernel(u_ref, m_ref, bst_ref, bsw_ref, cst_ref, av_ref, y_ref, sin_ref, sinw_ref, xp_ref):
    nb, nc, _ = sin_ref.shape
    ub = u_ref[0].astype(bf16)
    sin_ref[...] = _dot(ub, bst_ref[0]).reshape(nb, nc, 128)
    sinw_ref[...] = _dot(ub, bsw_ref[0]).reshape(nb, nc, 128)
    a1 = av_ref[0, 0:1, :]
    a2 = av_ref[0, 1:2, :]
    a2w = av_ref[0, 2:3, :]

    def step(c, carry):
        x, xw = carry
        xp_ref[:, pl.ds(c, 1), :] = x[:, None, :]
        s = sin_ref[:, pl.ds(c, 1), :][:, 0, :]
        sw = sinw_ref[:, pl.ds(c, 1), :][:, 0, :]
        return a1 * x + a2 * xw + s, a1 * xw + a2w * x + sw

    z = jnp.zeros((nb, 128), f32)
    lax.fori_loop(0, nc, step, (z, z))
    xprev = xp_ref[...].reshape(nb * nc, 128).astype(bf16)
    y = _dot(ub, m_ref[0]) + _dot(xprev, cst_ref[0])
    y_ref[0] = jax.nn.gelu(y)


def _s5_core(ut, m, bst, bsw, cst, av, nb):
    g, r, cols = ut.shape
    nc = r // nb
    return pl.pallas_call(
        _s5_kernel,
        grid=(g,),
        in_specs=[
            pl.BlockSpec((1, r, cols), lambda i: (i, 0, 0)),
            pl.BlockSpec((1, cols, cols), lambda i: (i, 0, 0)),
            pl.BlockSpec((1, cols, 128), lambda i: (i, 0, 0)),
            pl.BlockSpec((1, cols, 128), lambda i: (i, 0, 0)),
            pl.BlockSpec((1, 128, cols), lambda i: (i, 0, 0)),
            pl.BlockSpec((1, 8, 128), lambda i: (i, 0, 0)),
        ],
        out_specs=pl.BlockSpec((1, r, cols), lambda i: (i, 0, 0)),
        out_shape=jax.ShapeDtypeStruct((g, r, cols), f32),
        scratch_shapes=[pltpu.VMEM((nb, nc, 128), f32)] * 3,
        compiler_params=pltpu.CompilerParams(dimension_semantics=("parallel",)),
        name="s5_core",
    )(ut, m, bst, bsw, cst, av)


def _s5_prep(lam_re, lam_im, log_dt, b_re, b_im, c_re, c_im, d_skip):
    hp = lax.Precision.HIGHEST
    g, n, p = S5_GROUPS, S5_STATE, S5_P
    dt = jnp.exp(log_dt)[:, None]
    lr, li = lam_re, lam_im
    mag = jnp.exp(lr * dt)
    ar, ai = mag * jnp.cos(li * dt), mag * jnp.sin(li * dt)
    den = lr * lr + li * li
    cr = ((ar - 1.0) * lr + ai * li) / den
    ci = (ai * lr - (ar - 1.0) * li) / den
    bbr = cr[..., None] * b_re - ci[..., None] * b_im
    bbi = cr[..., None] * b_im + ci[..., None] * b_re
    tau = jnp.arange(S5_L + 1, dtype=f32)[:, None, None]
    pmag = jnp.exp(tau * (lr * dt)[None])
    pr, pi = pmag * jnp.cos(tau * (li * dt)[None]), pmag * jnp.sin(tau * (li * dt)[None])
    cpr = c_re[None] * pr[:, :, None, :] - c_im[None] * pi[:, :, None, :]
    cpi = c_re[None] * pi[:, :, None, :] + c_im[None] * pr[:, :, None, :]
    kern = (jnp.einsum("tgpn,gnq->tgpq", cpr[:S5_L], bbr, precision=hp)
            - jnp.einsum("tgpn,gnq->tgpq", cpi[:S5_L], bbi, precision=hp))
    kern = jnp.concatenate([kern, jnp.zeros((1, g, p, p), f32)], axis=0)
    s_idx = jnp.arange(S5_L)[:, None]
    t_idx = jnp.arange(S5_L)[None, :]
    lag = jnp.where(t_idx >= s_idx, t_idx - s_idx, S5_L)
    m = kern[lag]
    m = m.transpose(2, 0, 4, 1, 3).reshape(g, S5_COLS, S5_COLS)
    dvec = jnp.tile(d_skip.reshape(g, p), (1, S5_L))
    m = m + jnp.eye(S5_COLS, dtype=f32)[None] * dvec[:, None, :]
    prs, pis = pr[:S5_L][::-1], pi[:S5_L][::-1]
    bst_r = prs[:, :, :, None] * bbr[None] - pis[:, :, :, None] * bbi[None]
    bst_i = prs[:, :, :, None] * bbi[None] + pis[:, :, :, None] * bbr[None]
    bst_r = bst_r.transpose(1, 0, 3, 2).reshape(g, S5_COLS, n)
    bst_i = bst_i.transpose(1, 0, 3, 2).reshape(g, S5_COLS, n)
    bst = jnp.concatenate([bst_r, bst_i], axis=-1)
    bsw = jnp.concatenate([bst_i, bst_r], axis=-1)
    cst_r = cpr[1:].transpose(1, 3, 0, 2).reshape(g, n, S5_COLS)
    cst_i = -cpi[1:].transpose(1, 3, 0, 2).reshape(g, n, S5_COLS)
    cst = jnp.concatenate([cst_r, cst_i], axis=1)
    alr, ali = pr[S5_L], pi[S5_L]
    av = jnp.stack([jnp.concatenate([alr, alr], -1),
                    jnp.concatenate([-ali, ali], -1),
                    jnp.concatenate([ali, -ali], -1)], axis=1)
    av = jnp.concatenate([av, jnp.zeros((g, 5, 128), f32)], axis=1)
    return m.astype(bf16), bst.astype(bf16), bsw.astype(bf16), cst.astype(bf16), av


def _hgrn_kernel(q_ref, f_ref, i_ref, og_ref, lb_ref, ng_ref, y_ref, st_ref):
    tb = q_ref.shape[0]

    @pl.when(pl.program_id(2) == 0)
    def _():
        st_ref[...] = jnp.zeros_like(st_ref)

    lb = lb_ref[...]
    ng = ng_ref[...]
    row = lax.broadcasted_iota(jnp.int32, (CHUNK, CHUNK), 0)
    col = lax.broadcasted_iota(jnp.int32, (CHUNK, CHUNK), 1)
    tril = (row >= col).astype(bf16)
    ones_l = jnp.ones((HG_D, HG_D), bf16)
    nsq = HG_SUB * HG_SUB
    prow = lax.broadcasted_iota(jnp.int32, (HG_SUB, nsq), 0)
    pcol = lax.broadcasted_iota(jnp.int32, (HG_SUB, nsq), 1)
    pick = ((pcol >= prow * HG_SUB) & (pcol < (prow + 1) * HG_SUB)).astype(bf16)
    srow = lax.broadcasted_iota(jnp.int32, (HG_SUB, HG_D), 0)

    def chunk(c, carry):
        r0 = pl.multiple_of(c * CHUNK, CHUNK)
        q = jax.nn.silu(q_ref[pl.ds(r0, CHUNK), :])
        f = lb + (1.0 - lb) * jax.nn.sigmoid(f_ref[pl.ds(r0, CHUNK), :])
        logf = jnp.log(f)
        k = 1.0 - f
        v = i_ref[pl.ds(r0, CHUNK), :]
        cum = _exact_dot(tril, logf)
        last = cum[CHUNK - 1:CHUNK, :]
        st = st_ref[...]
        out = _dot_nt((q * jnp.exp(cum)).astype(bf16), st.astype(bf16))
        vb = v.astype(bf16)
        intra = []
        for i in range(CHUNK // HG_SUB):
            lo = i * HG_SUB
            qi, ki, vi, ci = q[lo:lo + HG_SUB], k[lo:lo + HG_SUB], v[lo:lo + HG_SUB], cum[lo:lo + HG_SUB]
            parts = []
            for t in range(HG_SUB):
                dlt = jnp.where(srow <= t, ci[t:t + 1, :] - ci, NEG)
                parts.append((qi[t:t + 1, :] * ki) * jnp.exp(dlt))
            amat = jnp.concatenate(parts, axis=0)
            rsum = _dot(amat.astype(bf16), ones_l)
            wv = rsum * jnp.concatenate([vi] * HG_SUB, axis=0)
            od = _dot(pick, wv.astype(bf16))
            if i > 0:
                ref_row = cum[lo - 1:lo, :]
                qd = (qi * jnp.exp(ci - ref_row)).astype(bf16)
                kd = (k[:lo] * jnp.exp(ref_row - cum[:lo])).astype(bf16)
                sc = _dot_nt(qd, kd)
                od = od + _dot(sc.astype(bf16), vb[:lo])
            intra.append(od)
        out = out + jnp.concatenate(intra, axis=0)
        kdec = (k * jnp.exp(last - cum)).astype(bf16)
        st_ref[...] = st * jnp.exp(last) + _dot_tn(vb, kdec)
        o = _rms(out, ng) * jax.nn.silu(og_ref[pl.ds(r0, CHUNK), :])
        y_ref[pl.ds(r0, CHUNK), :] = o.astype(y_ref.dtype)
        return carry

    lax.fori_loop(0, tb // CHUNK, chunk, 0)


def _hgrn(proj, lb, ng, nb, seq, tb):
    t = proj.shape[0]
    nj = seq // tb

    def seg(off):
        return pl.BlockSpec((tb, HG_D), lambda b, h, j, o=off // HG_D: (b * nj + j, o + h))

    par = pl.BlockSpec((1, HG_D), lambda b, h, j: (0, h))
    return pl.pallas_call(
        _hgrn_kernel,
        grid=(nb, HG_HEADS, nj),
        in_specs=[seg(OFF_Q), seg(OFF_F), seg(OFF_I), seg(OFF_OG), par, par],
        out_specs=pl.BlockSpec((tb, HG_D), lambda b, h, j: (b * nj + j, h)),
        out_shape=jax.ShapeDtypeStruct((t, HG_WIDTH), bf16),
        scratch_shapes=[pltpu.VMEM((HG_D, HG_D), f32)],
        compiler_params=pltpu.CompilerParams(dimension_semantics=("parallel", "parallel", "arbitrary")),
        name="hgrn2",
    )(proj, proj, proj, proj, lb, ng)


def _mlstm_kernel(cx_ref, v_ref, o_ref, gt_ref, gb_ref, cw_ref, cb_ref, wqk_ref, ng_ref, y_ref,
                  c_ref, m_ref, carry_ref, buf_ref, q_s, k_s):
    tb = cx_ref.shape[0]
    head = pl.program_id(1)

    @pl.when(pl.program_id(2) == 0)
    def _():
        c_ref[...] = jnp.zeros_like(c_ref)
        m_ref[...] = jnp.zeros_like(m_ref)
        carry_ref[...] = jnp.zeros_like(carry_ref)

    cx = cx_ref[...]
    buf_ref[0:8, :] = carry_ref[...]
    buf_ref[8:, :] = cx
    carry_ref[...] = cx[tb - 8:, :]
    ca = cw_ref[ML_CONV - 1:ML_CONV, :] * cx + cb_ref[...]
    for kk in range(ML_CONV - 1):
        sh = ML_CONV - 1 - kk
        ca = ca + cw_ref[kk:kk + 1, :] * buf_ref[pl.ds(8 - sh, tb), :]
    qk = _dot(jax.nn.silu(ca).astype(bf16), wqk_ref[0])
    q_s[...] = qk[:, :ML_DP].astype(bf16)
    k_s[...] = (qk[:, ML_DP:] * (ML_DH ** -0.5)).astype(bf16)

    gb = gb_ref[...]
    ng = ng_ref[...]
    row = lax.broadcasted_iota(jnp.int32, (CHUNK, CHUNK), 0)
    col = lax.broadcasted_iota(jnp.int32, (CHUNK, CHUNK), 1)
    causal = row >= col
    tril = causal.astype(bf16)
    lane = lax.broadcasted_iota(jnp.int32, (CHUNK, 128), 1)
    lane_w = lax.broadcasted_iota(jnp.int32, (CHUNK, ML_DP), 1)
    ones_c = jnp.ones((CHUNK, 128), bf16)

    def chunk(c, carry):
        r0 = pl.multiple_of(c * CHUNK, CHUNK)
        q = q_s[pl.ds(r0, CHUNK), :]
        k = k_s[pl.ds(r0, CHUNK), :]
        v = v_ref[pl.ds(r0, CHUNK), :]
        vaug = jnp.where(lane_w == ML_DH, 1.0, v).astype(bf16)
        gts = gt_ref[pl.ds(r0, CHUNK), :] + gb
        ib = jnp.sum(jnp.where(lane == head, gts, 0.0), axis=-1, keepdims=True)
        lf = jax.nn.log_sigmoid(jnp.sum(jnp.where(lane == head + ML_HEADS, gts, 0.0), axis=-1, keepdims=True))
        bcum = _exact_dot(tril, jnp.broadcast_to(lf, (CHUNK, 128)))[:, 0:1]
        zmat = jnp.where(lane == 0, ib - bcum, 0.0)
        zh, zm, zl = _split3(zmat)
        zrow = _dot_nt(ones_c, zh) + _dot_nt(ones_c, zm) + _dot_nt(ones_c, zl)
        dmat = jnp.where(causal, bcum + zrow, -jnp.inf)
        m_prev = m_ref[:, 0:1]
        inter_log = bcum + m_prev
        m_t = jnp.maximum(inter_log, jnp.max(dmat, axis=-1, keepdims=True))
        w_inter = jnp.exp(inter_log - m_t)
        w_intra = jnp.exp(dmat - m_t) * _dot_nt(q, k)
        cmat = c_ref[...]
        num = w_inter * _dot(q, cmat.astype(bf16)) + _dot(w_intra.astype(bf16), vaug)
        den = jnp.sum(jnp.where(lane_w == ML_DH, num, 0.0), axis=-1, keepdims=True)
        h = num / jnp.maximum(jnp.abs(den), jnp.exp(-m_t))
        m_new = m_t[CHUNK - 1:CHUNK, :]
        b_last = bcum[CHUNK - 1:CHUNK, :]
        decay = jnp.exp(b_last + m_prev - m_new)
        ws = jnp.exp(b_last - bcum + ib - m_new)
        c_ref[...] = decay * cmat + _dot_tn((ws * k.astype(f32)).astype(bf16), vaug)
        m_ref[...] = jnp.broadcast_to(m_new, m_ref.shape)
        hm = jnp.where(lane_w < ML_DH, h, 0.0)
        ms = jnp.sum(hm * hm, axis=-1, keepdims=True) * (1.0 / ML_DH)
        hn = hm * lax.rsqrt(ms + EPS) * ng
        y_ref[pl.ds(r0, CHUNK), :] = (hn * jax.nn.sigmoid(o_ref[pl.ds(r0, CHUNK), :])).astype(y_ref.dtype)
        return carry

    lax.fori_loop(0, tb // CHUNK, chunk, 0)


def _mlstm(proj, gb, cw, cb, wqk, ng, nb, seq, tb):
    t = proj.shape[0]
    nj = seq // tb

    def seg(off):
        return pl.BlockSpec((tb, ML_DP), lambda b, h, j, o=off // ML_DP: (b * nj + j, o + h))

    return pl.pallas_call(
        _mlstm_kernel,
        grid=(nb, ML_HEADS, nj),
        in_specs=[
            seg(OFF_CX), seg(OFF_V), seg(OFF_O),
            pl.BlockSpec((tb, 128), lambda b, h, j: (b * nj + j, OFF_IGFG // 128)),
            pl.BlockSpec((1, 128), lambda b, h, j: (0, 0)),
            pl.BlockSpec((ML_CONV, ML_DP), lambda b, h, j: (0, h)),
            pl.BlockSpec((1, ML_DP), lambda b, h, j: (0, h)),
            pl.BlockSpec((1, ML_DP, 2 * ML_DP), lambda b, h, j: (h, 0, 0)),
            pl.BlockSpec((1, ML_DP), lambda b, h, j: (0, h)),
        ],
        out_specs=pl.BlockSpec((tb, ML_DP), lambda b, h, j: (b * nj + j, h)),
        out_shape=jax.ShapeDtypeStruct((t, ML_WP), bf16),
        scratch_shapes=[
            pltpu.VMEM((ML_DP, ML_DP), f32),
            pltpu.VMEM((1, 128), f32),
            pltpu.VMEM((8, ML_DP), f32),
            pltpu.VMEM((tb + 8, ML_DP), f32),
            pltpu.VMEM((tb, ML_DP), bf16),
            pltpu.VMEM((tb, ML_DP), bf16),
        ],
        compiler_params=pltpu.CompilerParams(dimension_semantics=("parallel", "parallel", "arbitrary")),
        name="mlstm",
    )(proj, proj, proj, proj, gb, cw, cb, wqk, ng)


def _merge_kernel(g5_ref, yb_ref, yc_ref, sa_ref, sb_ref, sc_ref, wglu_ref, bglu_ref,
                  wa_ref, wb_ref, wc_ref, o_ref):
    g5 = g5_ref[...]
    ya = g5 * jax.nn.sigmoid(_dot(g5.astype(bf16), wglu_ref[...]) + bglu_ref[...])
    m = sa_ref[...].astype(f32) * _dot(ya.astype(bf16), wa_ref[...])
    m = m + sb_ref[...].astype(f32) * _dot(yb_ref[...], wb_ref[...])
    m = m + sc_ref[...].astype(f32) * _dot(yc_ref[...], wc_ref[...])
    o_ref[...] = m.astype(o_ref.dtype)


def _merge(g5, yb, yc, sg, wglu, bglu, wa, wb, wc, tm):
    t = g5.shape[0]
    d = D_MODEL

    def rows(w):
        return pl.BlockSpec((tm, w), lambda i: (i, 0))

    def full(a):
        return pl.BlockSpec(a.shape, lambda i: (0, 0))

    return pl.pallas_call(
        _merge_kernel,
        grid=(t // tm,),
        in_specs=[
            rows(S5_WIDTH), rows(HG_WIDTH), rows(ML_WP),
            pl.BlockSpec((tm, d), lambda i: (i, 0)),
            pl.BlockSpec((tm, d), lambda i: (i, 1)),
            pl.BlockSpec((tm, d), lambda i: (i, 2)),
            full(wglu), full(bglu), full(wa), full(wb), full(wc),
        ],
        out_specs=rows(d),
        out_shape=jax.ShapeDtypeStruct((t, d), bf16),
        compiler_params=pltpu.CompilerParams(dimension_semantics=("parallel",)),
        name="merge",
    )(g5, yb, yc, sg, sg, sg, wglu, bglu, wa, wb, wc)


def _proj_res_kernel(a_ref, w_ref, r_ref, o_ref):
    o_ref[...] = r_ref[...] + _dot(a_ref[...], w_ref[...])


def _proj_res(a, w, res, tm):
    t, k = a.shape
    n = w.shape[1]
    return pl.pallas_call(
        _proj_res_kernel,
        grid=(t // tm,),
        in_specs=[
            pl.BlockSpec((tm, k), lambda i: (i, 0)),
            pl.BlockSpec((k, n), lambda i: (0, 0)),
            pl.BlockSpec((tm, n), lambda i: (i, 0)),
        ],
        out_specs=pl.BlockSpec((tm, n), lambda i: (i, 0)),
        out_shape=jax.ShapeDtypeStruct((t, n), f32),
        compiler_params=pltpu.CompilerParams(dimension_semantics=("parallel",)),
        name="proj_res",
    )(a, w, res)


def _ffn_kernel(x_ref, g_ref, wa_ref, wb_ref, cwa_ref, cwb_ref, cba_ref, cbb_ref, wd_ref, o_ref,
                hs_ref, acc_ref, bufa_ref, bufb_ref, cara_ref, carb_ref, *, tiles_per_seq):
    tm = x_ref.shape[0]
    i = pl.program_id(0)
    j = pl.program_id(1)

    @pl.when(j == 0)
    def _():
        hs_ref[...] = _rms(x_ref[...], g_ref[...]).astype(bf16)
        acc_ref[...] = jnp.zeros_like(acc_ref)

    @pl.when(i % tiles_per_seq == 0)
    def _():
        cara_ref[j] = jnp.zeros((8, FFN_TN), f32)
        carb_ref[j] = jnp.zeros((8, FFN_TN), f32)

    hs = hs_ref[...]

    def conv(w_ref, cw_ref, cb_ref, buf_ref, car_ref):
        u = _dot(hs, w_ref[...])
        buf_ref[0:8, :] = car_ref[j]
        buf_ref[8:, :] = u
        car_ref[j] = u[tm - 8:, :]
        y = cw_ref[FFN_CONV - 1:FFN_CONV, :] * u + cb_ref[...]
        for kk in range(FFN_CONV - 1):
            sh = FFN_CONV - 1 - kk
            y = y + cw_ref[kk:kk + 1, :] * buf_ref[pl.ds(8 - sh, tm), :]
        return y

    a = conv(wa_ref, cwa_ref, cba_ref, bufa_ref, cara_ref)
    b = conv(wb_ref, cwb_ref, cbb_ref, bufb_ref, carb_ref)
    act = (jax.nn.silu(a) * b).astype(bf16)
    acc_ref[...] += _dot(act, wd_ref[...])

    @pl.when(j == FFN_NJ - 1)
    def _():
        o_ref[...] = x_ref[...] + acc_ref[...]


def _ffn(x, g, wup, cw, cb, wd, seq, tm):
    t, d = x.shape
    return pl.pallas_call(
        functools.partial(_ffn_kernel, tiles_per_seq=seq // tm),
        grid=(t // tm, FFN_NJ),
        in_specs=[
            pl.BlockSpec((tm, d), lambda i, j: (i, 0)),
            pl.BlockSpec((1, d), lambda i, j: (0, 0)),
            pl.BlockSpec((d, FFN_TN), lambda i, j: (0, j)),
            pl.BlockSpec((d, FFN_TN), lambda i, j: (0, j + FFN_NJ)),
            pl.BlockSpec((FFN_CONV, FFN_TN), lambda i, j: (0, j)),
            pl.BlockSpec((FFN_CONV, FFN_TN), lambda i, j: (0, j + FFN_NJ)),
            pl.BlockSpec((1, FFN_TN), lambda i, j: (0, j)),
            pl.BlockSpec((1, FFN_TN), lambda i, j: (0, j + FFN_NJ)),
            pl.BlockSpec((FFN_TN, d), lambda i, j: (j, 0)),
        ],
        out_specs=pl.BlockSpec((tm, d), lambda i, j: (i, 0)),
        out_shape=jax.ShapeDtypeStruct((t, d), f32),
        scratch_shapes=[
            pltpu.VMEM((tm, d), bf16),
            pltpu.VMEM((tm, d), f32),
            pltpu.VMEM((tm + 8, FFN_TN), f32),
            pltpu.VMEM((tm + 8, FFN_TN), f32),
            pltpu.VMEM((FFN_NJ, 8, FFN_TN), f32),
            pltpu.VMEM((FFN_NJ, 8, FFN_TN), f32),
        ],
        compiler_params=pltpu.CompilerParams(dimension_semantics=("arbitrary", "arbitrary")),
        name="ffn",
    )(x, g, wup, wup, cw, cw, cb, cb, wd)


def _rms_kernel(x_ref, g_ref, o_ref):
    o_ref[...] = _rms(x_ref[...], g_ref[...])


def _final_norm(x, g, tm):
    t, d = x.shape
    return pl.pallas_call(
        _rms_kernel,
        grid=(t // tm,),
        in_specs=[pl.BlockSpec((tm, d), lambda i: (i, 0)), pl.BlockSpec((1, d), lambda i: (0, 0))],
        out_specs=pl.BlockSpec((tm, d), lambda i: (i, 0)),
        out_shape=jax.ShapeDtypeStruct((t, d), f32),
        compiler_params=pltpu.CompilerParams(dimension_semantics=("parallel",)),
        name="final_norm",
    )(x, g)


def _pad_heads(w, axis):
    shp = w.shape
    w = w.reshape(shp[:axis] + (ML_HEADS, ML_DH) + shp[axis + 1:])
    pad = [(0, 0)] * w.ndim
    pad[axis + 1] = (0, ML_DP - ML_DH)
    w = jnp.pad(w, pad)
    return w.reshape(shp[:axis] + (ML_WP,) + shp[axis + 1:])


def _pack_w_in(w):
    d = w.shape[0]
    parts = [
        w[:, 0:3584],
        _pad_heads(w[:, 3584:4352], 1),
        _pad_heads(w[:, 4352:5120], 1),
        _pad_heads(w[:, 5120:5888], 1),
        w[:, 5888:5896],
        jnp.zeros((d, MIXER_COLS - OFF_IGFG - 2 * ML_HEADS), w.dtype),
    ]
    return jnp.concatenate(parts, axis=1).astype(bf16), w[:, 5896:].astype(bf16)


def _layer(x, p, nb, seq, tiles):
    t = x.shape[0]
    proj = _norm_proj(x, p["mix_norm"], p["w_mixer"], act="none", out_dtype=f32,
                      tm=tiles["tm_proj"], tn=tiles["tn_proj"])
    sg = _norm_proj(x, p["mix_norm"], p["w_gates"], act="sigmoid", out_dtype=bf16,
                    tm=tiles["tm_proj"], tn=tiles["tn_proj"])
    nc = seq // S5_L
    ut = proj[:, :S5_WIDTH].reshape(nb, nc, S5_L, S5_GROUPS, S5_P)
    ut = ut.transpose(3, 0, 1, 2, 4).reshape(S5_GROUPS, nb * nc, S5_COLS)
    g5t = _s5_core(ut, p["s5_m"], p["s5_bst"], p["s5_bsw"], p["s5_cst"], p["s5_av"], nb)
    g5 = g5t.reshape(S5_GROUPS, nb, nc, S5_L, S5_P).transpose(1, 2, 3, 0, 4).reshape(t, S5_WIDTH)
    yb = _hgrn(proj, p["hg_lb"], p["hg_norm"], nb, seq, tiles["tb"])
    yc = _mlstm(proj, p["ml_gb"], p["ml_cw"], p["ml_cb"], p["ml_wqk"], p["ml_norm"], nb, seq, tiles["tb"])
    merged = _merge(g5, yb, yc, sg, p["s5_wglu"], p["s5_bglu"], p["wb_a"], p["wb_b"], p["wb_c"], tiles["tm_merge"])
    x = _proj_res(merged, p["w_out"], x, tiles["tm_merge"])
    return _ffn(x, p["ffn_norm"], p["ffn_wup"], p["ffn_cw"], p["ffn_cb"], p["ffn_wd"], seq, tiles["tm_ffn"])


def _tiles(seq):
    return {
        "tm_proj": min(1024, seq), "tn_proj": 1024,
        "tb": min(512, seq), "tm_merge": min(512, seq), "tm_ffn": min(512, seq),
    }


def kernel(x, mix_norm, w_in, s5_lam_re, s5_lam_im, s5_log_dt, s5_b_re, s5_b_im, s5_c_re, s5_c_im, s5_d, s5_w_glu, s5_b_glu, hg_lower_bounds, hg_norm, ml_conv_w, ml_conv_b, ml_w_qk, ml_b_ig, ml_b_fg, ml_norm, w_branch, w_out, ffn_norm, ffn_w_up, ffn_conv_w, ffn_conv_b, ffn_w_down, final_norm):
    nb, seq, d = x.shape
    depth = w_in.shape[0]
    tiles = _tiles(seq)
    lbs = jax.nn.softmax(hg_lower_bounds.astype(f32), axis=0)
    lbs = jnp.cumsum(lbs, axis=0) - lbs[0:1]
    xs = x.astype(f32).reshape(nb * seq, d)
    for l in range(depth):
        w_mixer, w_gates = _pack_w_in(w_in[l])
        s5_m, s5_bst, s5_bsw, s5_cst, s5_av = _s5_prep(
            s5_lam_re[l], s5_lam_im[l], s5_log_dt[l], s5_b_re[l], s5_b_im[l], s5_c_re[l], s5_c_im[l], s5_d[l])
        wqk = ml_w_qk[l]
        wq = jnp.pad(wqk[:, :, :ML_DH], ((0, 0), (0, ML_DP - ML_DH), (0, ML_DP - ML_DH)))
        wk = jnp.pad(wqk[:, :, ML_DH:], ((0, 0), (0, ML_DP - ML_DH), (0, ML_DP - ML_DH)))
        gbias = jnp.concatenate([ml_b_ig[l], ml_b_fg[l], jnp.zeros((128 - 2 * ML_HEADS,), f32)])[None, :]
        wb = w_branch[l]
        p = {
            "mix_norm": mix_norm[l][None, :], "w_mixer": w_mixer, "w_gates": w_gates,
            "s5_m": s5_m, "s5_bst": s5_bst, "s5_bsw": s5_bsw, "s5_cst": s5_cst, "s5_av": s5_av,
            "s5_wglu": s5_w_glu[l].astype(bf16), "s5_bglu": s5_b_glu[l][None, :],
            "hg_lb": lbs[l][None, :], "hg_norm": hg_norm[l][None, :],
            "ml_gb": gbias, "ml_cw": _pad_heads(ml_conv_w[l], 1), "ml_cb": _pad_heads(ml_conv_b[l][None, :], 1),
            "ml_wqk": jnp.concatenate([wq, wk], axis=-1).astype(bf16),
            "ml_norm": _pad_heads(ml_norm[l][None, :], 1),
            "wb_a": wb[:S5_WIDTH].astype(bf16), "wb_b": wb[S5_WIDTH:S5_WIDTH + HG_WIDTH].astype(bf16),
            "wb_c": _pad_heads(wb[S5_WIDTH + HG_WIDTH:], 0).astype(bf16),
            "w_out": w_out[l].astype(bf16),
            "ffn_norm": ffn_norm[l][None, :], "ffn_wup": ffn_w_up[l].astype(bf16),
            "ffn_cw": ffn_conv_w[l], "ffn_cb": ffn_conv_b[l][None, :], "ffn_wd": ffn_w_down[l].astype(bf16),
        }
        xs = _layer(xs, p, nb, seq, tiles)
    out = _final_norm(xs, final_norm[None, :], tiles["tm_merge"])
    return out.reshape(nb, seq, d).astype(x.dtype)
```

```python
import functools
import math

import jax
import jax.numpy as jnp
from jax import lax
from jax.experimental import pallas as pl
from jax.experimental.pallas import tpu as pltpu

f32 = jnp.float32
bf16 = jnp.bfloat16

D_MODEL = 2048
EPS = 1e-6
CHUNK = 64

S5_GROUPS = 32
S5_P = 16
S5_WIDTH = 512
S5_STATE = 64
S5_L = 8
S5_OCT = 8

HG_HEADS = 6
HG_D = 128
HG_WIDTH = 768
HG_SUB = 8

ML_HEADS = 4
ML_DH = 192
ML_DP = 256
ML_WIDTH = 768
ML_WP = ML_HEADS * ML_DP
ML_CONV = 4

FFN_DIM = 5632
FFN_CONV = 3
FFN_TN = 512
FFN_NJ = FFN_DIM // FFN_TN

OFF_U, OFF_Q, OFF_F, OFF_I, OFF_OG = 0, 512, 1280, 2048, 2816
OFF_CX, OFF_V, OFF_O, OFF_IGFG = 3584, 4608, 5632, 6656
MIXER_COLS = 7168
GATE_COLS = 3 * D_MODEL

NEG = -1e30
VMEM_LIMIT = 60 * 1024 * 1024


def _split3(x):
    hi = x.astype(bf16)
    r1 = x - hi.astype(f32)
    mid = r1.astype(bf16)
    lo = (r1 - mid.astype(f32)).astype(bf16)
    return hi, mid, lo


def _dot(a, b):
    return jnp.dot(a, b, preferred_element_type=f32)


def _dot_nt(a, b):
    return lax.dot_general(a, b, (((1,), (1,)), ((), ())), preferred_element_type=f32)


def _dot_tn(a, b):
    return lax.dot_general(a, b, (((0,), (0,)), ((), ())), preferred_element_type=f32)


def _exact_dot(a_bf16, x):
    hi, mid, lo = _split3(x)
    return _dot(a_bf16, hi) + _dot(a_bf16, mid) + _dot(a_bf16, lo)


def _rms(x, g):
    ms = jnp.mean(x * x, axis=-1, keepdims=True)
    return x * lax.rsqrt(ms + EPS) * g


def _norm_proj_kernel(x_ref, g_ref, w_ref, o_ref, hs_ref, *, act):
    @pl.when(pl.program_id(1) == 0)
    def _():
        hs_ref[...] = _rms(x_ref[...], g_ref[...]).astype(bf16)

    y = _dot(hs_ref[...], w_ref[...])
    if act == "sigmoid":
        y = jax.nn.sigmoid(y)
    o_ref[...] = y.astype(o_ref.dtype)


def _norm_proj(x, g, w, *, act, out_dtype, tm, tn):
    t, d = x.shape
    n = w.shape[1]
    return pl.pallas_call(
        functools.partial(_norm_proj_kernel, act=act),
        grid=(t // tm, n // tn),
        in_specs=[
            pl.BlockSpec((tm, d), lambda i, j: (i, 0)),
            pl.BlockSpec((1, d), lambda i, j: (0, 0)),
            pl.BlockSpec((d, tn), lambda i, j: (0, j)),
        ],
        out_specs=pl.BlockSpec((tm, tn), lambda i, j: (i, j)),
        out_shape=jax.ShapeDtypeStruct((t, n), out_dtype),
        scratch_shapes=[pltpu.VMEM((tm, d), bf16)],
        compiler_params=pltpu.CompilerParams(dimension_semantics=("parallel", "arbitrary")),
        name="norm_proj_" + act,
    )(x, g, w)


def _s5_kernel(u_ref, m_ref, bst_ref, cst_ref, av_ref, y_ref, sin_ref, xp_ref):
    nc = u_ref.shape[0]
    half = S5_OCT * S5_STATE
    xb = jnp.concatenate([u_ref[:, t, :] for t in range(S5_L)], axis=1).astype(bf16)
    sin_ref[...] = _dot(xb, bst_ref[0])
    ar = av_ref[0, 0:1, :]
    ai = av_ref[0, 1:2, :]

    def step(c, carry):
        xr, xi = carry
        xp_ref[pl.ds(c, 1), :] = jnp.concatenate([xr, xi], axis=1)
        s = sin_ref[pl.ds(c, 1), :]
        return ar * xr - ai * xi + s[:, :half], ar * xi + ai * xr + s[:, half:]

    z = jnp.zeros((1, half), f32)
    lax.fori_loop(0, nc, step, (z, z), unroll=8)
    y = jax.nn.gelu(_dot(xb, m_ref[0]) + _dot(xp_ref[...].astype(bf16), cst_ref[0]))
    for t in range(S5_L):
        y_ref[:, t, :] = y[:, t * 128:(t + 1) * 128]


def _s5_core(proj3, m, bst, cst, av, nb):
    r, _, _ = proj3.shape
    nc = r // nb
    noct = S5_GROUPS // S5_OCT
    cols = S5_L * 128
    st = 2 * S5_OCT * S5_STATE
    return pl.pallas_call(
        _s5_kernel,
        grid=(noct, nb),
        in_specs=[
            pl.BlockSpec((nc, S5_L, 128), lambda o, b: (b, 0, OFF_U // 128 + o)),
            pl.BlockSpec((1, cols, cols), lambda o, b: (o, 0, 0)),
            pl.BlockSpec((1, cols, st), lambda o, b: (o, 0, 0)),
            pl.BlockSpec((1, st, cols), lambda o, b: (o, 0, 0)),
            pl.BlockSpec((1, 8, st // 2), lambda o, b: (o, 0, 0)),
        ],
        out_specs=pl.BlockSpec((nc, S5_L, 128), lambda o, b: (b, 0, o)),
        out_shape=jax.ShapeDtypeStruct((r, S5_L, S5_WIDTH), f32),
        scratch_shapes=[pltpu.VMEM((nc, st), f32), pltpu.VMEM((nc, st), f32)],
        compiler_params=pltpu.CompilerParams(dimension_semantics=("parallel", "parallel")),
        name="s5_core",
    )(proj3, m, bst, cst, av)


def _s5_prep(lam_re, lam_im, log_dt, b_re, b_im, c_re, c_im, d_skip):
    hp = lax.Precision.HIGHEST
    g, n, p, ln, oc = S5_GROUPS, S5_STATE, S5_P, S5_L, S5_OCT
    no = g // oc
    dt = jnp.exp(log_dt)[:, None]
    lr, li = lam_re, lam_im
    mag = jnp.exp(lr * dt)
    ar, ai = mag * jnp.cos(li * dt), mag * jnp.sin(li * dt)
    den = lr * lr + li * li
    cr = ((ar - 1.0) * lr + ai * li) / den
    ci = (ai * lr - (ar - 1.0) * li) / den
    bbr = cr[..., None] * b_re - ci[..., None] * b_im
    bbi = cr[..., None] * b_im + ci[..., None] * b_re
    tau = jnp.arange(ln + 1, dtype=f32)[:, None, None]
    pmag = jnp.exp(tau * (lr * dt)[None])
    pr, pi = pmag * jnp.cos(tau * (li * dt)[None]), pmag * jnp.sin(tau * (li * dt)[None])
    cpr = c_re[None] * pr[:, :, None, :] - c_im[None] * pi[:, :, None, :]
    cpi = c_re[None] * pi[:, :, None, :] + c_im[None] * pr[:, :, None, :]
    kern = (jnp.einsum("tgpn,gnq->tgpq", cpr[:ln], bbr, precision=hp)
            - jnp.einsum("tgpn,gnq->tgpq", cpi[:ln], bbi, precision=hp))
    kern = jnp.concatenate([kern, jnp.zeros((1, g, p, p), f32)], axis=0)
    s_idx = jnp.arange(ln)[:, None]
    t_idx = jnp.arange(ln)[None, :]
    lag = jnp.where(t_idx >= s_idx, t_idx - s_idx, ln)
    m = kern[lag]
    dskip = d_skip.reshape(g, p)
    m = m + (jnp.eye(ln, dtype=f32)[:, :, None, None, None] * jnp.eye(p, dtype=f32)[None, None, None]
             * dskip[None, None, :, :, None])
    eye = jnp.eye(oc, dtype=f32)
    m = m.transpose(2, 0, 4, 1, 3).reshape(no, oc, ln, p, ln, p)
    m8 = m.transpose(0, 2, 1, 3, 4, 5)[:, :, :, :, :, None, :] * eye[None, None, :, None, None, :, None]
    m8 = m8.reshape(no, ln * oc * p, ln * oc * p)
    tau_b = (ln - 1) - tau[:ln]
    bmag = jnp.exp(tau_b * (lr * dt)[None])
    prs, pis = bmag * jnp.cos(tau_b * (li * dt)[None]), bmag * jnp.sin(tau_b * (li * dt)[None])
    bst_r = prs[:, :, :, None] * bbr[None] - pis[:, :, :, None] * bbi[None]
    bst_i = prs[:, :, :, None] * bbi[None] + pis[:, :, :, None] * bbr[None]
    bst = jnp.stack([bst_r, bst_i], axis=0)
    bst = bst.reshape(2, ln, no, oc, n, p).transpose(2, 1, 3, 5, 0, 4)
    bst8 = bst[:, :, :, :, :, None, :] * eye[None, None, :, None, None, :, None]
    bst8 = bst8.reshape(no, ln * oc * p, 2 * oc * n)
    cst = jnp.stack([cpr[1:], -cpi[1:]], axis=0)
    cst = cst.reshape(2, ln, no, oc, p, n).transpose(2, 0, 3, 5, 1, 4)
    cst8 = cst[:, :, :, :, :, None, :] * eye[None, None, :, None, None, :, None]
    cst8 = cst8.reshape(no, 2 * oc * n, ln * oc * p)
    av = jnp.stack([pr[ln].reshape(no, oc * n), pi[ln].reshape(no, oc * n)], axis=1)
    av = jnp.concatenate([av, jnp.zeros((no, 6, oc * n), f32)], axis=1)
    return m8.astype(bf16), bst8.astype(bf16), cst8.astype(bf16), av


def _hgrn_kernel(q_ref, f_ref, i_ref, og_ref, lb_ref, ng_ref, y_ref, st_ref, k_s, cum_s):
    tb = q_ref.shape[0]
    nblk = CHUNK // HG_SUB

    @pl.when(pl.program_id(2) == 0)
    def _():
        st_ref[...] = jnp.zeros_like(st_ref)

    lb = lb_ref[...]
    ng = ng_ref[...]
    row = lax.broadcasted_iota(jnp.int32, (CHUNK, CHUNK), 0)
    col = lax.broadcasted_iota(jnp.int32, (CHUNK, CHUNK), 1)
    tril = (row >= col).astype(bf16)
    diag_mask = ((row // HG_SUB) == (col // HG_SUB)) & (col <= row)
    levels = []
    half = HG_SUB
    while half < CHUNK:
        levels.append((half, ((row // (2 * half)) == (col // (2 * half)))
                       & ((row & half) != 0) & ((col & half) == 0)))
        half *= 2
    wrow = lax.broadcasted_iota(jnp.int32, (HG_SUB * HG_D, CHUNK), 0)
    wcol = lax.broadcasted_iota(jnp.int32, (HG_SUB * HG_D, CHUNK), 1)
    wsel = ((wcol % HG_SUB) == (wrow // HG_D)).astype(bf16)

    f = lb + (1.0 - lb) * jax.nn.sigmoid(f_ref[...])
    k_s[...] = 1.0 - f
    logf = jnp.log(f)
    for c in range(tb // CHUNK):
        cum_s[c * CHUNK:(c + 1) * CHUNK, :] = _exact_dot(tril, logf[c * CHUNK:(c + 1) * CHUNK])

    st = st_ref[...]
    for c in range(tb // CHUNK):
        r0 = c * CHUNK
        q = jax.nn.silu(q_ref[r0:r0 + CHUNK, :])
        k = k_s[r0:r0 + CHUNK, :]
        cum = cum_s[r0:r0 + CHUNK, :]
        vb = i_ref[r0:r0 + CHUNK, :].astype(bf16)
        last = cum_s[r0 + CHUNK - 1:r0 + CHUNK, :]
        inter = _dot_nt((q * jnp.exp(cum)).astype(bf16), st.astype(bf16))
        slabs = []
        for s in range(HG_SUB):
            pieces = []
            for b in range(nblk):
                rr = r0 + b * HG_SUB + s
                blk = slice(b * HG_SUB, (b + 1) * HG_SUB)
                dec = jnp.exp(jnp.minimum(cum[blk] - cum_s[rr:rr + 1, :], 0.0))
                pieces.append((q[blk] * k_s[rr:rr + 1, :]) * dec)
            slabs.append(jnp.concatenate(pieces, axis=0).astype(bf16))
        sc = jnp.where(diag_mask, _dot(jnp.concatenate(slabs, axis=1), wsel), 0.0)
        for half, mask in levels:
            refs = [jnp.broadcast_to(cum_s[r0 + base + half - 1:r0 + base + half, :], (2 * half, HG_D))
                    for base in range(0, CHUNK, 2 * half)]
            dlt = cum - jnp.concatenate(refs, axis=0)
            e = jnp.exp(jnp.minimum(dlt, -dlt))
            sc = jnp.where(mask, _dot_nt((q * e).astype(bf16), (k * e).astype(bf16)), sc)
        out = inter + _dot(sc.astype(bf16), vb)
        kdec = (k * jnp.exp(last - cum)).astype(bf16)
        st = st * jnp.exp(last) + _dot_tn(vb, kdec)
        o = _rms(out, ng) * jax.nn.silu(og_ref[r0:r0 + CHUNK, :])
        y_ref[r0:r0 + CHUNK, :] = o.astype(y_ref.dtype)
    st_ref[...] = st


def _hgrn(proj, lb, ng, nb, seq, tb):
    t = proj.shape[0]
    nj = seq // tb

    def seg(off):
        return pl.BlockSpec((tb, HG_D), lambda b, h, j, o=off // HG_D: (b * nj + j, o + h))

    par = pl.BlockSpec((1, HG_D), lambda b, h, j: (0, h))
    return pl.pallas_call(
        _hgrn_kernel,
        grid=(nb, HG_HEADS, nj),
        in_specs=[seg(OFF_Q), seg(OFF_F), seg(OFF_I), seg(OFF_OG), par, par],
        out_specs=pl.BlockSpec((tb, HG_D), lambda b, h, j: (b * nj + j, h)),
        out_shape=jax.ShapeDtypeStruct((t, HG_WIDTH), bf16),
        scratch_shapes=[pltpu.VMEM((HG_D, HG_D), f32), pltpu.VMEM((tb, HG_D), f32), pltpu.VMEM((tb, HG_D), f32)],
        compiler_params=pltpu.CompilerParams(dimension_semantics=("parallel", "parallel", "arbitrary")),
        name="hgrn2",
    )(proj, proj, proj, proj, lb, ng)


def _mlstm_kernel(cx_ref, v_ref, o_ref, gt_ref, gb_ref, cw_ref, cb_ref, wqk_ref, ng_ref, y_ref,
                  c_ref, m_ref, carry_ref, buf_ref, q_s, k_s):
    tb = cx_ref.shape[0]
    head = pl.program_id(1)

    @pl.when(pl.program_id(2) == 0)
    def _():
        c_ref[...] = jnp.zeros_like(c_ref)
        m_ref[...] = jnp.zeros_like(m_ref)
        carry_ref[...] = jnp.zeros_like(carry_ref)

    gb = gb_ref[...]
    ng = ng_ref[...]
    row = lax.broadcasted_iota(jnp.int32, (CHUNK, CHUNK), 0)
    col = lax.broadcasted_iota(jnp.int32, (CHUNK, CHUNK), 1)
    causal = row >= col
    tril = causal.astype(bf16)
    lane = lax.broadcasted_iota(jnp.int32, (CHUNK, 128), 1)
    lane_w = lax.broadcasted_iota(jnp.int32, (CHUNK, ML_DP), 1)

    nch = tb // CHUNK
    lane_b = lax.broadcasted_iota(jnp.int32, (tb, 128), 1)
    gts = gt_ref[...] + gb
    ib_all = jnp.sum(jnp.where(lane_b == head, gts, 0.0), axis=-1, keepdims=True)
    lf_all = jax.nn.log_sigmoid(jnp.sum(jnp.where(lane_b == head + ML_HEADS, gts, 0.0), axis=-1, keepdims=True))

    lfc = jnp.zeros((CHUNK, 128), f32)
    ibc = jnp.zeros((CHUNK, 128), f32)
    for c in range(nch):
        lfc = jnp.where(lane == c, lf_all[c * CHUNK:(c + 1) * CHUNK], lfc)
        ibc = jnp.where(lane == c, ib_all[c * CHUNK:(c + 1) * CHUNK], ibc)
    bc = _exact_dot(tril, lfc)
    zt = (ibc - bc).T

    ibs, bcums, dmats, dmaxs = [], [], [], []
    for c in range(nch):
        bcum = bc[:, c:c + 1]
        dmat = jnp.where(causal, bcum + zt[c:c + 1, :], -jnp.inf)
        ibs.append(ib_all[c * CHUNK:(c + 1) * CHUNK])
        bcums.append(bcum)
        dmats.append(dmat)
        dmaxs.append(jnp.max(dmat, axis=-1, keepdims=True))

    m_prev = m_ref[:, 0:1]
    m_prevs, m_ts = [], []
    for c in range(nch):
        m_t = jnp.maximum(bcums[c] + m_prev, dmaxs[c])
        m_prevs.append(m_prev)
        m_ts.append(m_t)
        m_prev = m_t[CHUNK - 1:CHUNK, :]
    m_ref[...] = jnp.broadcast_to(m_prev, m_ref.shape)

    cx = cx_ref[...]
    buf_ref[0:8, :] = carry_ref[...]
    buf_ref[8:, :] = cx
    carry_ref[...] = cx[tb - 8:, :]
    ca = cw_ref[ML_CONV - 1:ML_CONV, :] * cx + cb_ref[...]
    for kk in range(ML_CONV - 1):
        sh = ML_CONV - 1 - kk
        ca = ca + cw_ref[kk:kk + 1, :] * buf_ref[pl.ds(8 - sh, tb), :]
    qk = _dot(jax.nn.silu(ca).astype(bf16), wqk_ref[0])
    q_s[...] = qk[:, :ML_DP].astype(bf16)
    k_s[...] = (qk[:, ML_DP:] * (ML_DH ** -0.5)).astype(bf16)

    qs, w_inters, intras, decays, kvs = [], [], [], [], []
    for c in range(nch):
        r0 = c * CHUNK
        q = q_s[r0:r0 + CHUNK, :]
        k = k_s[r0:r0 + CHUNK, :]
        vaug = jnp.where(lane_w == ML_DH, 1.0, v_ref[r0:r0 + CHUNK, :]).astype(bf16)
        qs.append(q)
        m_t, bcum = m_ts[c], bcums[c]
        m_new = m_t[CHUNK - 1:CHUNK, :]
        b_last = bcum[CHUNK - 1:CHUNK, :]
        w_inters.append(jnp.exp(bcum + m_prevs[c] - m_t))
        w_intra = jnp.exp(dmats[c] - m_t) * _dot_nt(q, k)
        intras.append(_dot(w_intra.astype(bf16), vaug))
        decays.append(jnp.exp(b_last + m_prevs[c] - m_new))
        ws = jnp.exp(b_last - bcum + ibs[c] - m_new)
        kvs.append(_dot_tn((ws * k.astype(f32)).astype(bf16), vaug))

    cmat = c_ref[...]
    nums = []
    for c in range(nch):
        nums.append(w_inters[c] * _dot(qs[c], cmat.astype(bf16)) + intras[c])
        cmat = decays[c] * cmat + kvs[c]
    c_ref[...] = cmat

    for c in range(nch):
        r0 = c * CHUNK
        num = nums[c]
        den = jnp.sum(jnp.where(lane_w == ML_DH, num, 0.0), axis=-1, keepdims=True)
        h = num / jnp.maximum(jnp.abs(den), jnp.exp(-m_ts[c]))
        hm = jnp.where(lane_w < ML_DH, h, 0.0)
        ms = jnp.sum(hm * hm, axis=-1, keepdims=True) * (1.0 / ML_DH)
        hn = hm * lax.rsqrt(ms + EPS) * ng
        y_ref[r0:r0 + CHUNK, :] = (hn * jax.nn.sigmoid(o_ref[r0:r0 + CHUNK, :])).astype(y_ref.dtype)


def _mlstm(proj, gb, cw, cb, wqk, ng, nb, seq, tb):
    t = proj.shape[0]
    nj = seq // tb

    def seg(off):
        return pl.BlockSpec((tb, ML_DP), lambda b, h, j, o=off // ML_DP: (b * nj + j, o + h))

    return pl.pallas_call(
        _mlstm_kernel,
        grid=(nb, ML_HEADS, nj),
        in_specs=[
            seg(OFF_CX), seg(OFF_V), seg(OFF_O),
            pl.BlockSpec((tb, 128), lambda b, h, j: (b * nj + j, OFF_IGFG // 128)),
            pl.BlockSpec((1, 128), lambda b, h, j: (0, 0)),
            pl.BlockSpec((ML_CONV, ML_DP), lambda b, h, j: (0, h)),
            pl.BlockSpec((1, ML_DP), lambda b, h, j: (0, h)),
            pl.BlockSpec((1, ML_DP, 2 * ML_DP), lambda b, h, j: (h, 0, 0)),
            pl.BlockSpec((1, ML_DP), lambda b, h, j: (0, h)),
        ],
        out_specs=pl.BlockSpec((tb, ML_DP), lambda b, h, j: (b * nj + j, h)),
        out_shape=jax.ShapeDtypeStruct((t, ML_WP), bf16),
        scratch_shapes=[
            pltpu.VMEM((ML_DP, ML_DP), f32),
            pltpu.VMEM((1, 128), f32),
            pltpu.VMEM((8, ML_DP), f32),
            pltpu.VMEM((tb + 8, ML_DP), f32),
            pltpu.VMEM((tb, ML_DP), bf16),
            pltpu.VMEM((tb, ML_DP), bf16),
        ],
        compiler_params=pltpu.CompilerParams(dimension_semantics=("parallel", "parallel", "arbitrary")),
        name="mlstm",
    )(proj, proj, proj, proj, gb, cw, cb, wqk, ng)


def _merge_kernel(g5_ref, yb_ref, yc_ref, sa_ref, sb_ref, sc_ref, wglu_ref, bglu_ref,
                  wa_ref, wb_ref, wc_ref, o_ref):
    g5 = g5_ref[...]
    ya = g5 * jax.nn.sigmoid(_dot(g5.astype(bf16), wglu_ref[...]) + bglu_ref[...])
    m = sa_ref[...].astype(f32) * _dot(ya.astype(bf16), wa_ref[...])
    m = m + sb_ref[...].astype(f32) * _dot(yb_ref[...], wb_ref[...])
    m = m + sc_ref[...].astype(f32) * _dot(yc_ref[...], wc_ref[...])
    o_ref[...] = m.astype(o_ref.dtype)


def _merge(g5, yb, yc, sg, wglu, bglu, wa, wb, wc, tm):
    t = g5.shape[0]
    d = D_MODEL

    def rows(w):
        return pl.BlockSpec((tm, w), lambda i: (i, 0))

    def full(a):
        return pl.BlockSpec(a.shape, lambda i: (0, 0))

    return pl.pallas_call(
        _merge_kernel,
        grid=(t // tm,),
        in_specs=[
            rows(S5_WIDTH), rows(HG_WIDTH), rows(ML_WP),
            pl.BlockSpec((tm, d), lambda i: (i, 0)),
            pl.BlockSpec((tm, d), lambda i: (i, 1)),
            pl.BlockSpec((tm, d), lambda i: (i, 2)),
            full(wglu), full(bglu), full(wa), full(wb), full(wc),
        ],
        out_specs=rows(d),
        out_shape=jax.ShapeDtypeStruct((t, d), bf16),
        compiler_params=pltpu.CompilerParams(dimension_semantics=("parallel",)),
        name="merge",
    )(g5, yb, yc, sg, sg, sg, wglu, bglu, wa, wb, wc)


def _proj_res_kernel(a_ref, w_ref, r_ref, o_ref):
    o_ref[...] = r_ref[...] + _dot(a_ref[...], w_ref[...])


def _proj_res(a, w, res, tm):
    t, k = a.shape
    n = w.shape[1]
    return pl.pallas_call(
        _proj_res_kernel,
        grid=(t // tm,),
        in_specs=[
            pl.BlockSpec((tm, k), lambda i: (i, 0)),
            pl.BlockSpec((k, n), lambda i: (0, 0)),
            pl.BlockSpec((tm, n), lambda i: (i, 0)),
        ],
        out_specs=pl.BlockSpec((tm, n), lambda i: (i, 0)),
        out_shape=jax.ShapeDtypeStruct((t, n), f32),
        compiler_params=pltpu.CompilerParams(dimension_semantics=("parallel",)),
        name="proj_res",
    )(a, w, res)


def _ffn_kernel(x_ref, g_ref, wa_ref, wb_ref, cwa_ref, cwb_ref, cba_ref, cbb_ref, wd_ref, o_ref,
                hs_ref, bufa_ref, bufb_ref, cara_ref, carb_ref, *, tiles_per_seq):
    tm = x_ref.shape[0]
    i = pl.program_id(0)
    j = pl.program_id(1)

    @pl.when(j == 0)
    def _():
        x = x_ref[...]
        hs_ref[...] = _rms(x, g_ref[...]).astype(bf16)
        o_ref[...] = x

    @pl.when(i % tiles_per_seq == 0)
    def _():
        cara_ref[j] = jnp.zeros((8, FFN_TN), f32)
        carb_ref[j] = jnp.zeros((8, FFN_TN), f32)

    hs = hs_ref[...]

    def conv(w_ref, cw_ref, cb_ref, buf_ref, car_ref):
        u = _dot(hs, w_ref[...])
        buf_ref[0:8, :] = car_ref[j]
        buf_ref[8:, :] = u
        car_ref[j] = u[tm - 8:, :]
        y = cw_ref[FFN_CONV - 1:FFN_CONV, :] * u + cb_ref[...]
        for kk in range(FFN_CONV - 1):
            sh = FFN_CONV - 1 - kk
            y = y + cw_ref[kk:kk + 1, :] * buf_ref[pl.ds(8 - sh, tm), :]
        return y

    a = conv(wa_ref, cwa_ref, cba_ref, bufa_ref, cara_ref)
    b = conv(wb_ref, cwb_ref, cbb_ref, bufb_ref, carb_ref)
    act = (jax.nn.silu(a) * b).astype(bf16)
    o_ref[...] += _dot(act, wd_ref[...])


def _ffn(x, g, wup, cw, cb, wd, seq, tm):
    t, d = x.shape
    return pl.pallas_call(
        functools.partial(_ffn_kernel, tiles_per_seq=seq // tm),
        grid=(t // tm, FFN_NJ),
        in_specs=[
            pl.BlockSpec((tm, d), lambda i, j: (i, 0), pipeline_mode=pl.Buffered(1)),
            pl.BlockSpec((1, d), lambda i, j: (0, 0)),
            pl.BlockSpec((d, FFN_TN), lambda i, j: (0, j)),
            pl.BlockSpec((d, FFN_TN), lambda i, j: (0, j + FFN_NJ)),
            pl.BlockSpec((FFN_CONV, FFN_TN), lambda i, j: (0, j)),
            pl.BlockSpec((FFN_CONV, FFN_TN), lambda i, j: (0, j + FFN_NJ)),
            pl.BlockSpec((1, FFN_TN), lambda i, j: (0, j)),
            pl.BlockSpec((1, FFN_TN), lambda i, j: (0, j + FFN_NJ)),
            pl.BlockSpec((FFN_TN, d), lambda i, j: (j, 0)),
        ],
        out_specs=pl.BlockSpec((tm, d), lambda i, j: (i, 0)),
        out_shape=jax.ShapeDtypeStruct((t, d), f32),
        scratch_shapes=[
            pltpu.VMEM((tm, d), bf16),
            pltpu.VMEM((tm + 8, FFN_TN), f32),
            pltpu.VMEM((tm + 8, FFN_TN), f32),
            pltpu.VMEM((FFN_NJ, 8, FFN_TN), f32),
            pltpu.VMEM((FFN_NJ, 8, FFN_TN), f32),
        ],
        compiler_params=pltpu.CompilerParams(dimension_semantics=("arbitrary", "arbitrary"),
                                             vmem_limit_bytes=VMEM_LIMIT),
        name="ffn",
    )(x, g, wup, wup, cw, cw, cb, cb, wd)


def _rms_kernel(x_ref, g_ref, o_ref):
    o_ref[...] = _rms(x_ref[...], g_ref[...])


def _final_norm(x, g, tm):
    t, d = x.shape
    return pl.pallas_call(
        _rms_kernel,
        grid=(t // tm,),
        in_specs=[pl.BlockSpec((tm, d), lambda i: (i, 0)), pl.BlockSpec((1, d), lambda i: (0, 0))],
        out_specs=pl.BlockSpec((tm, d), lambda i: (i, 0)),
        out_shape=jax.ShapeDtypeStruct((t, d), f32),
        compiler_params=pltpu.CompilerParams(dimension_semantics=("parallel",)),
        name="final_norm",
    )(x, g)


def _pad_heads(w, axis):
    shp = w.shape
    w = w.reshape(shp[:axis] + (ML_HEADS, ML_DH) + shp[axis + 1:])
    pad = [(0, 0)] * w.ndim
    pad[axis + 1] = (0, ML_DP - ML_DH)
    w = jnp.pad(w, pad)
    return w.reshape(shp[:axis] + (ML_WP,) + shp[axis + 1:])


def _pack_w_in(w):
    d = w.shape[0]
    w = w.astype(bf16)
    parts = [
        w[:, 0:3584],
        _pad_heads(w[:, 3584:4352], 1),
        _pad_heads(w[:, 4352:5120], 1),
        _pad_heads(w[:, 5120:5888], 1),
        w[:, 5888:5896],
        jnp.zeros((d, MIXER_COLS - OFF_IGFG - 2 * ML_HEADS), w.dtype),
    ]
    return jnp.concatenate(parts, axis=1), w[:, 5896:]


def _layer(x, p, nb, seq, tiles):
    t = x.shape[0]
    proj = _norm_proj(x, p["mix_norm"], p["w_mixer"], act="none", out_dtype=f32,
                      tm=tiles["tm_proj"], tn=tiles["tn_proj"])
    sg = _norm_proj(x, p["mix_norm"], p["w_gates"], act="sigmoid", out_dtype=bf16,
                    tm=tiles["tm_proj"], tn=tiles["tn_proj"])
    g5 = _s5_core(proj.reshape(t // S5_L, S5_L, MIXER_COLS), p["s5_m"], p["s5_bst"], p["s5_cst"], p["s5_av"], nb)
    g5 = g5.reshape(t, S5_WIDTH)
    yb = _hgrn(proj, p["hg_lb"], p["hg_norm"], nb, seq, tiles["tb"])
    yc = _mlstm(proj, p["ml_gb"], p["ml_cw"], p["ml_cb"], p["ml_wqk"], p["ml_norm"], nb, seq, tiles["tb"])
    merged = _merge(g5, yb, yc, sg, p["s5_wglu"], p["s5_bglu"], p["wb_a"], p["wb_b"], p["wb_c"], tiles["tm_merge"])
    x = _proj_res(merged, p["w_out"], x, tiles["tm_merge"])
    return _ffn(x, p["ffn_norm"], p["ffn_wup"], p["ffn_cw"], p["ffn_cb"], p["ffn_wd"], seq, tiles["tm_ffn"])


def _tiles(seq):
    return {
        "tm_proj": min(1024, seq), "tn_proj": 1024,
        "tb": min(512, seq), "tm_merge": min(512, seq), "tm_ffn": min(1024, seq),
    }


def kernel(x, mix_norm, w_in, s5_lam_re, s5_lam_im, s5_log_dt, s5_b_re, s5_b_im, s5_c_re, s5_c_im, s5_d, s5_w_glu, s5_b_glu, hg_lower_bounds, hg_norm, ml_conv_w, ml_conv_b, ml_w_qk, ml_b_ig, ml_b_fg, ml_norm, w_branch, w_out, ffn_norm, ffn_w_up, ffn_conv_w, ffn_conv_b, ffn_w_down, final_norm):
    nb, seq, d = x.shape
    depth = w_in.shape[0]
    tiles = _tiles(seq)
    lbs = jax.nn.softmax(hg_lower_bounds.astype(f32), axis=0)
    lbs = jnp.cumsum(lbs, axis=0) - lbs[0:1]
    xs = x.astype(f32).reshape(nb * seq, d)
    for l in range(depth):
        w_mixer, w_gates = _pack_w_in(w_in[l])
        s5_m, s5_bst, s5_cst, s5_av = _s5_prep(
            s5_lam_re[l], s5_lam_im[l], s5_log_dt[l], s5_b_re[l], s5_b_im[l], s5_c_re[l], s5_c_im[l], s5_d[l])
        wqk = ml_w_qk[l]
        wq = jnp.pad(wqk[:, :, :ML_DH], ((0, 0), (0, ML_DP - ML_DH), (0, ML_DP - ML_DH)))
        wk = jnp.pad(wqk[:, :, ML_DH:], ((0, 0), (0, ML_DP - ML_DH), (0, ML_DP - ML_DH)))
        gbias = jnp.concatenate([ml_b_ig[l], ml_b_fg[l], jnp.zeros((128 - 2 * ML_HEADS,), f32)])[None, :]
        wb = w_branch[l]
        p = {
            "mix_norm": mix_norm[l][None, :], "w_mixer": w_mixer, "w_gates": w_gates,
            "s5_m": s5_m, "s5_bst": s5_bst, "s5_cst": s5_cst, "s5_av": s5_av,
            "s5_wglu": s5_w_glu[l].astype(bf16), "s5_bglu": s5_b_glu[l][None, :],
            "hg_lb": lbs[l][None, :], "hg_norm": hg_norm[l][None, :],
            "ml_gb": gbias, "ml_cw": _pad_heads(ml_conv_w[l], 1), "ml_cb": _pad_heads(ml_conv_b[l][None, :], 1),
            "ml_wqk": jnp.concatenate([wq, wk], axis=-1).astype(bf16),
            "ml_norm": _pad_heads(ml_norm[l][None, :], 1),
            "wb_a": wb[:S5_WIDTH].astype(bf16), "wb_b": wb[S5_WIDTH:S5_WIDTH + HG_WIDTH].astype(bf16),
            "wb_c": _pad_heads(wb[S5_WIDTH + HG_WIDTH:], 0).astype(bf16),
            "w_out": w_out[l].astype(bf16),
            "ffn_norm": ffn_norm[l][None, :], "ffn_wup": ffn_w_up[l].astype(bf16),
            "ffn_cw": ffn_conv_w[l], "ffn_cb": ffn_conv_b[l][None, :], "ffn_wd": ffn_w_down[l].astype(bf16),
        }
        xs = _layer(xs, p, nb, seq, tiles)
    out = _final_norm(xs, final_norm[None, :], tiles["tm_merge"])
    return out.reshape(nb, seq, d).astype(x.dtype)
```

```python
import functools

import jax
import jax.numpy as jnp
from jax import lax
from jax.experimental import pallas as pl
from jax.experimental.pallas import tpu as pltpu

f32 = jnp.float32
bf16 = jnp.bfloat16

D_MODEL = 2048
EPS = 1e-6
CHUNK = 64

S5_GROUPS = 32
S5_P = 16
S5_WIDTH = 512
S5_STATE = 64
S5_L = 8
S5_OCT = 8

HG_HEADS = 6
HG_D = 128
HG_WIDTH = 768
HG_SUB = 8

ML_HEADS = 4
ML_DH = 192
ML_DP = 256
ML_WIDTH = 768
ML_WP = ML_HEADS * ML_DP
ML_CONV = 4

FFN_DIM = 5632
FFN_CONV = 3
FFN_TN = 512
FFN_NJ = FFN_DIM // FFN_TN

OFF_U, OFF_Q, OFF_F, OFF_I, OFF_OG = 0, 512, 1280, 2048, 2816
OFF_CX, OFF_V, OFF_O, OFF_IGFG = 3584, 4608, 5632, 6656
MIXER_COLS = 7168
GATE_COLS = 3 * D_MODEL

VMEM_LIMIT = 60 * 1024 * 1024


def _split3(x):
    hi = x.astype(bf16)
    r1 = x - hi.astype(f32)
    mid = r1.astype(bf16)
    lo = (r1 - mid.astype(f32)).astype(bf16)
    return hi, mid, lo


def _dot(a, b):
    return jnp.dot(a, b, preferred_element_type=f32)


def _dot_nt(a, b):
    return lax.dot_general(a, b, (((1,), (1,)), ((), ())), preferred_element_type=f32)


def _dot_tn(a, b):
    return lax.dot_general(a, b, (((0,), (0,)), ((), ())), preferred_element_type=f32)


def _exact_dot(a_bf16, x):
    hi, mid, lo = _split3(x)
    return _dot(a_bf16, hi) + _dot(a_bf16, mid) + _dot(a_bf16, lo)


def _rms(x, g):
    ms = jnp.mean(x * x, axis=-1, keepdims=True)
    return x * lax.rsqrt(ms + EPS) * g


def _norm_proj_kernel(x_ref, g_ref, w_ref, o_ref, hs_ref, *, act):
    @pl.when(pl.program_id(1) == 0)
    def _():
        hs_ref[...] = _rms(x_ref[...], g_ref[...]).astype(bf16)

    y = _dot(hs_ref[...], w_ref[...])
    if act == "sigmoid":
        y = jax.nn.sigmoid(y)
    o_ref[...] = y.astype(o_ref.dtype)


def _norm_proj(x, g, w, *, act, out_dtype, tm, tn):
    t, d = x.shape
    n = w.shape[1]
    return pl.pallas_call(
        functools.partial(_norm_proj_kernel, act=act),
        grid=(t // tm, n // tn),
        in_specs=[
            pl.BlockSpec((tm, d), lambda i, j: (i, 0)),
            pl.BlockSpec((1, d), lambda i, j: (0, 0)),
            pl.BlockSpec((d, tn), lambda i, j: (0, j)),
        ],
        out_specs=pl.BlockSpec((tm, tn), lambda i, j: (i, j)),
        out_shape=jax.ShapeDtypeStruct((t, n), out_dtype),
        scratch_shapes=[pltpu.VMEM((tm, d), bf16)],
        compiler_params=pltpu.CompilerParams(dimension_semantics=("parallel", "arbitrary"),
                                             vmem_limit_bytes=VMEM_LIMIT),
        name="norm_proj_" + act,
    )(x, g, w)


def _s5_kernel(u_ref, m_ref, bst_ref, cst_ref, av_ref, y_ref, sin_ref, xp_ref):
    nc = u_ref.shape[0] // S5_L
    half = S5_OCT * S5_STATE
    xb = jnp.concatenate([u_ref[pl.ds(t, nc, stride=S5_L), :] for t in range(S5_L)], axis=1).astype(bf16)
    sin_ref[...] = _dot(xb, bst_ref[0])
    ar = av_ref[0, 0:1, :]
    ai = av_ref[0, 1:2, :]

    def step(c, carry):
        xr, xi = carry
        xp_ref[pl.ds(c, 1), :] = jnp.concatenate([xr, xi], axis=1)
        s = sin_ref[pl.ds(c, 1), :]
        return ar * xr - ai * xi + s[:, :half], ar * xi + ai * xr + s[:, half:]

    z = jnp.zeros((1, half), f32)
    lax.fori_loop(0, nc, step, (z, z), unroll=8)
    y = jax.nn.gelu(_dot(xb, m_ref[0]) + _dot(xp_ref[...].astype(bf16), cst_ref[0]))
    for t in range(S5_L):
        y_ref[pl.ds(t, nc, stride=S5_L), :] = y[:, t * 128:(t + 1) * 128]


def _s5_core(proj, m, bst, cst, av, nb):
    t = proj.shape[0]
    seq = t // nb
    nc = seq // S5_L
    noct = S5_GROUPS // S5_OCT
    cols = S5_L * 128
    st = 2 * S5_OCT * S5_STATE
    return pl.pallas_call(
        _s5_kernel,
        grid=(noct, nb),
        in_specs=[
            pl.BlockSpec((seq, 128), lambda o, b: (b, OFF_U // 128 + o)),
            pl.BlockSpec((1, cols, cols), lambda o, b: (o, 0, 0)),
            pl.BlockSpec((1, cols, st), lambda o, b: (o, 0, 0)),
            pl.BlockSpec((1, st, cols), lambda o, b: (o, 0, 0)),
            pl.BlockSpec((1, 8, st // 2), lambda o, b: (o, 0, 0)),
        ],
        out_specs=pl.BlockSpec((seq, 128), lambda o, b: (b, o)),
        out_shape=jax.ShapeDtypeStruct((t, S5_WIDTH), f32),
        scratch_shapes=[pltpu.VMEM((nc, st), f32), pltpu.VMEM((nc, st), f32)],
        compiler_params=pltpu.CompilerParams(dimension_semantics=("parallel", "parallel")),
        name="s5_core",
    )(proj, m, bst, cst, av)


def _s5_prep(lam_re, lam_im, log_dt, b_re, b_im, c_re, c_im, d_skip):
    hp = lax.Precision.HIGHEST
    g, n, p, ln, oc = S5_GROUPS, S5_STATE, S5_P, S5_L, S5_OCT
    no = g // oc
    dt = jnp.exp(log_dt)[:, None]
    lr, li = lam_re, lam_im
    mag = jnp.exp(lr * dt)
    ar, ai = mag * jnp.cos(li * dt), mag * jnp.sin(li * dt)
    den = lr * lr + li * li
    cr = ((ar - 1.0) * lr + ai * li) / den
    ci = (ai * lr - (ar - 1.0) * li) / den
    bbr = cr[..., None] * b_re - ci[..., None] * b_im
    bbi = cr[..., None] * b_im + ci[..., None] * b_re
    tau = jnp.arange(ln + 1, dtype=f32)[:, None, None]
    pmag = jnp.exp(tau * (lr * dt)[None])
    pr, pi = pmag * jnp.cos(tau * (li * dt)[None]), pmag * jnp.sin(tau * (li * dt)[None])
    cpr = c_re[None] * pr[:, :, None, :] - c_im[None] * pi[:, :, None, :]
    cpi = c_re[None] * pi[:, :, None, :] + c_im[None] * pr[:, :, None, :]
    kern = (jnp.einsum("tgpn,gnq->tgpq", cpr[:ln], bbr, precision=hp)
            - jnp.einsum("tgpn,gnq->tgpq", cpi[:ln], bbi, precision=hp))
    dskip = d_skip.reshape(g, p)
    kern = kern.at[0].add(jnp.eye(p, dtype=f32)[None] * dskip[:, :, None])
    kern = jnp.concatenate([kern, jnp.zeros((1, g, p, p), f32)], axis=0)
    eye = jnp.eye(oc, dtype=f32)
    blk = kern.reshape(ln + 1, no, oc, p, p).transpose(0, 1, 2, 4, 3)
    blk = (blk[:, :, :, :, None, :] * eye[None, None, :, None, :, None]).reshape(ln + 1, no, oc * p, oc * p)
    s_idx = jnp.arange(ln)[:, None]
    t_idx = jnp.arange(ln)[None, :]
    lag = jnp.where(t_idx >= s_idx, t_idx - s_idx, ln)
    m8 = blk[lag].transpose(2, 0, 3, 1, 4).reshape(no, ln * oc * p, ln * oc * p)
    tau_b = (ln - 1) - tau[:ln]
    bmag = jnp.exp(tau_b * (lr * dt)[None])
    prs, pis = bmag * jnp.cos(tau_b * (li * dt)[None]), bmag * jnp.sin(tau_b * (li * dt)[None])
    bbrt, bbit = bbr.transpose(0, 2, 1), bbi.transpose(0, 2, 1)
    bst_r = prs[:, :, None, :] * bbrt[None] - pis[:, :, None, :] * bbit[None]
    bst_i = prs[:, :, None, :] * bbit[None] + pis[:, :, None, :] * bbrt[None]

    def expand_b(x):
        x = x.reshape(ln, no, oc, p, n)
        return (x[:, :, :, :, None, :] * eye[None, None, :, None, :, None]).reshape(ln, no, oc * p, oc * n)

    bst8 = jnp.concatenate([expand_b(bst_r), expand_b(bst_i)], axis=-1)
    bst8 = bst8.transpose(1, 0, 2, 3).reshape(no, ln * oc * p, 2 * oc * n)
    cst = jnp.stack([cpr[1:], -cpi[1:]], axis=0)
    cst = cst.reshape(2, ln, no, oc, p, n).transpose(0, 1, 2, 3, 5, 4)
    cst8 = cst[:, :, :, :, :, None, :] * eye[None, None, None, :, None, :, None]
    cst8 = cst8.reshape(2, ln, no, oc * n, oc * p).transpose(2, 0, 3, 1, 4).reshape(no, 2 * oc * n, ln * oc * p)
    av = jnp.stack([pr[ln].reshape(no, oc * n), pi[ln].reshape(no, oc * n)], axis=1)
    av = jnp.concatenate([av, jnp.zeros((no, 6, oc * n), f32)], axis=1)
    return m8.astype(bf16), bst8.astype(bf16), cst8.astype(bf16), av


def _hgrn_kernel(q_ref, f_ref, i_ref, og_ref, lb_ref, ng_ref, y_ref, st_ref, k_s, cum_s):
    tb = q_ref.shape[0]
    nblk = CHUNK // HG_SUB

    @pl.when(pl.program_id(2) == 0)
    def _():
        st_ref[...] = jnp.zeros_like(st_ref)

    lb = lb_ref[...]
    ng = ng_ref[...]
    row = lax.broadcasted_iota(jnp.int32, (CHUNK, CHUNK), 0)
    col = lax.broadcasted_iota(jnp.int32, (CHUNK, CHUNK), 1)
    tril = (row >= col).astype(bf16)
    diag_mask = ((row // HG_SUB) == (col // HG_SUB)) & (col <= row)
    levels = []
    half = HG_SUB
    while half < CHUNK:
        levels.append((half, ((row // (2 * half)) == (col // (2 * half)))
                       & ((row & half) != 0) & ((col & half) == 0)))
        half *= 2
    wrow = lax.broadcasted_iota(jnp.int32, (HG_SUB * HG_D, CHUNK), 0)
    wcol = lax.broadcasted_iota(jnp.int32, (HG_SUB * HG_D, CHUNK), 1)
    wsel = ((wcol % HG_SUB) == (wrow // HG_D)).astype(bf16)

    f = lb + (1.0 - lb) * jax.nn.sigmoid(f_ref[...])
    k_s[...] = 1.0 - f
    logf = jnp.log2(f)
    for c in range(tb // CHUNK):
        cum_s[c * CHUNK:(c + 1) * CHUNK, :] = _exact_dot(tril, logf[c * CHUNK:(c + 1) * CHUNK])

    st = st_ref[...]
    for c in range(tb // CHUNK):
        r0 = c * CHUNK
        q = jax.nn.silu(q_ref[r0:r0 + CHUNK, :])
        k = k_s[r0:r0 + CHUNK, :]
        cum = cum_s[r0:r0 + CHUNK, :]
        vb = i_ref[r0:r0 + CHUNK, :].astype(bf16)
        last = cum_s[r0 + CHUNK - 1:r0 + CHUNK, :]
        inter = _dot_nt((q * jnp.exp2(cum)).astype(bf16), st.astype(bf16))
        slabs = []
        for s in range(HG_SUB):
            pieces = []
            for b in range(nblk):
                rr = r0 + b * HG_SUB + s
                blk = slice(b * HG_SUB, (b + 1) * HG_SUB)
                dec = jnp.exp2(jnp.minimum(cum[blk] - cum_s[rr:rr + 1, :], 0.0))
                pieces.append((q[blk] * k_s[rr:rr + 1, :]) * dec)
            slabs.append(jnp.concatenate(pieces, axis=0).astype(bf16))
        sc = jnp.where(diag_mask, _dot(jnp.concatenate(slabs, axis=1), wsel), 0.0)
        for half, mask in levels:
            zs = []
            for base in range(0, CHUNK, 2 * half):
                mid = cum_s[r0 + base + half - 1:r0 + base + half, :]
                lo, hi = slice(base, base + half), slice(base + half, base + 2 * half)
                zs.append(k[lo] * jnp.exp2(mid - cum[lo]))
                zs.append(q[hi] * jnp.exp2(cum[hi] - mid))
            z = jnp.concatenate(zs, axis=0).astype(bf16)
            sc = jnp.where(mask, _dot_nt(z, z), sc)
        out = inter + _dot(sc.astype(bf16), vb)
        kdec = (k * jnp.exp2(last - cum)).astype(bf16)
        st = st * jnp.exp2(last) + _dot_tn(vb, kdec)
        o = _rms(out, ng) * jax.nn.silu(og_ref[r0:r0 + CHUNK, :])
        y_ref[r0:r0 + CHUNK, :] = o.astype(y_ref.dtype)
    st_ref[...] = st


def _hgrn(proj, lb, ng, nb, seq, tb):
    t = proj.shape[0]
    nj = seq // tb

    def seg(off):
        return pl.BlockSpec((tb, HG_D), lambda b, h, j, o=off // HG_D: (b * nj + j, o + h))

    par = pl.BlockSpec((1, HG_D), lambda b, h, j: (0, h))
    return pl.pallas_call(
        _hgrn_kernel,
        grid=(nb, HG_HEADS, nj),
        in_specs=[seg(OFF_Q), seg(OFF_F), seg(OFF_I), seg(OFF_OG), par, par],
        out_specs=pl.BlockSpec((tb, HG_D), lambda b, h, j: (b * nj + j, h)),
        out_shape=jax.ShapeDtypeStruct((t, HG_WIDTH), bf16),
        scratch_shapes=[pltpu.VMEM((HG_D, HG_D), f32), pltpu.VMEM((tb, HG_D), f32), pltpu.VMEM((tb, HG_D), f32)],
        compiler_params=pltpu.CompilerParams(dimension_semantics=("parallel", "parallel", "arbitrary")),
        name="hgrn2",
    )(proj, proj, proj, proj, lb, ng)


def _mlstm_kernel(cx_ref, v_ref, o_ref, gt_ref, gb_ref, cw_ref, cb_ref, wqk_ref, ng_ref, y_ref,
                  c_ref, m_ref, carry_ref, buf_ref, q_s, k_s):
    tb = cx_ref.shape[0]
    head = pl.program_id(1)

    @pl.when(pl.program_id(2) == 0)
    def _():
        c_ref[...] = jnp.zeros_like(c_ref)
        m_ref[...] = jnp.zeros_like(m_ref)
        carry_ref[...] = jnp.zeros_like(carry_ref)

    gb = gb_ref[...]
    ng = ng_ref[...]
    row = lax.broadcasted_iota(jnp.int32, (CHUNK, CHUNK), 0)
    col = lax.broadcasted_iota(jnp.int32, (CHUNK, CHUNK), 1)
    causal = row >= col
    tril = causal.astype(bf16)
    lane = lax.broadcasted_iota(jnp.int32, (CHUNK, 128), 1)
    lane_w = lax.broadcasted_iota(jnp.int32, (CHUNK, ML_DP), 1)

    nch = tb // CHUNK
    lane_b = lax.broadcasted_iota(jnp.int32, (tb, 128), 1)
    gts = gt_ref[...] + gb
    ib_all = jnp.sum(jnp.where(lane_b == head, gts, 0.0), axis=-1, keepdims=True)
    lf_all = jax.nn.log_sigmoid(jnp.sum(jnp.where(lane_b == head + ML_HEADS, gts, 0.0), axis=-1, keepdims=True))

    lfc = jnp.zeros((CHUNK, 128), f32)
    ibc = jnp.zeros((CHUNK, 128), f32)
    for c in range(nch):
        lfc = jnp.where(lane == c, lf_all[c * CHUNK:(c + 1) * CHUNK], lfc)
        ibc = jnp.where(lane == c, ib_all[c * CHUNK:(c + 1) * CHUNK], ibc)
    bc = _exact_dot(tril, lfc)
    zt = (ibc - bc).T

    ibs, bcums, dmats, dmaxs = [], [], [], []
    for c in range(nch):
        bcum = bc[:, c:c + 1]
        dmat = jnp.where(causal, bcum + zt[c:c + 1, :], -jnp.inf)
        ibs.append(ib_all[c * CHUNK:(c + 1) * CHUNK])
        bcums.append(bcum)
        dmats.append(dmat)
        dmaxs.append(jnp.max(dmat, axis=-1, keepdims=True))

    m_prev = m_ref[:, 0:1]
    m_prevs, m_ts = [], []
    for c in range(nch):
        m_t = jnp.maximum(bcums[c] + m_prev, dmaxs[c])
        m_prevs.append(m_prev)
        m_ts.append(m_t)
        m_prev = m_t[CHUNK - 1:CHUNK, :]
    m_ref[...] = jnp.broadcast_to(m_prev, m_ref.shape)

    cx = cx_ref[...]
    buf_ref[0:8, :] = carry_ref[...]
    buf_ref[8:, :] = cx
    carry_ref[...] = cx[tb - 8:, :]
    ca = cw_ref[ML_CONV - 1:ML_CONV, :] * cx + cb_ref[...]
    for kk in range(ML_CONV - 1):
        sh = ML_CONV - 1 - kk
        ca = ca + cw_ref[kk:kk + 1, :] * buf_ref[pl.ds(8 - sh, tb), :]
    qk = _dot(jax.nn.silu(ca).astype(bf16), wqk_ref[0])
    q_s[...] = qk[:, :ML_DP].astype(bf16)
    k_s[...] = (qk[:, ML_DP:] * (ML_DH ** -0.5)).astype(bf16)

    qs, w_inters, intras, decays, kvs = [], [], [], [], []
    for c in range(nch):
        r0 = c * CHUNK
        q = q_s[r0:r0 + CHUNK, :]
        k = k_s[r0:r0 + CHUNK, :]
        vaug = jnp.where(lane_w == ML_DH, 1.0, v_ref[r0:r0 + CHUNK, :]).astype(bf16)
        qs.append(q)
        m_t, bcum = m_ts[c], bcums[c]
        m_new = m_t[CHUNK - 1:CHUNK, :]
        b_last = bcum[CHUNK - 1:CHUNK, :]
        w_inters.append(jnp.exp(bcum + m_prevs[c] - m_t))
        w_intra = jnp.exp(dmats[c] - m_t) * _dot_nt(q, k)
        intras.append(_dot(w_intra.astype(bf16), vaug))
        decays.append(jnp.exp(b_last + m_prevs[c] - m_new))
        ws = jnp.exp(b_last - bcum + ibs[c] - m_new)
        kvs.append(_dot_tn((ws * k.astype(f32)).astype(bf16), vaug))

    cmat = c_ref[...]
    nums = []
    for c in range(nch):
        nums.append(w_inters[c] * _dot(qs[c], cmat.astype(bf16)) + intras[c])
        cmat = decays[c] * cmat + kvs[c]
    c_ref[...] = cmat

    for c in range(nch):
        r0 = c * CHUNK
        num = nums[c]
        den = jnp.sum(jnp.where(lane_w == ML_DH, num, 0.0), axis=-1, keepdims=True)
        h = num / jnp.maximum(jnp.abs(den), jnp.exp(-m_ts[c]))
        hm = jnp.where(lane_w < ML_DH, h, 0.0)
        ms = jnp.sum(hm * hm, axis=-1, keepdims=True) * (1.0 / ML_DH)
        hn = hm * lax.rsqrt(ms + EPS) * ng
        y_ref[r0:r0 + CHUNK, :] = (hn * jax.nn.sigmoid(o_ref[r0:r0 + CHUNK, :])).astype(y_ref.dtype)


def _mlstm(proj, gb, cw, cb, wqk, ng, nb, seq, tb):
    t = proj.shape[0]
    nj = seq // tb

    def seg(off):
        return pl.BlockSpec((tb, ML_DP), lambda b, h, j, o=off // ML_DP: (b * nj + j, o + h))

    return pl.pallas_call(
        _mlstm_kernel,
        grid=(nb, ML_HEADS, nj),
        in_specs=[
            seg(OFF_CX), seg(OFF_V), seg(OFF_O),
            pl.BlockSpec((tb, 128), lambda b, h, j: (b * nj + j, OFF_IGFG // 128)),
            pl.BlockSpec((1, 128), lambda b, h, j: (0, 0)),
            pl.BlockSpec((ML_CONV, ML_DP), lambda b, h, j: (0, h)),
            pl.BlockSpec((1, ML_DP), lambda b, h, j: (0, h)),
            pl.BlockSpec((1, ML_DP, 2 * ML_DP), lambda b, h, j: (h, 0, 0)),
            pl.BlockSpec((1, ML_DP), lambda b, h, j: (0, h)),
        ],
        out_specs=pl.BlockSpec((tb, ML_DP), lambda b, h, j: (b * nj + j, h)),
        out_shape=jax.ShapeDtypeStruct((t, ML_WP), bf16),
        scratch_shapes=[
            pltpu.VMEM((ML_DP, ML_DP), f32),
            pltpu.VMEM((1, 128), f32),
            pltpu.VMEM((8, ML_DP), f32),
            pltpu.VMEM((tb + 8, ML_DP), f32),
            pltpu.VMEM((tb, ML_DP), bf16),
            pltpu.VMEM((tb, ML_DP), bf16),
        ],
        compiler_params=pltpu.CompilerParams(dimension_semantics=("parallel", "parallel", "arbitrary")),
        name="mlstm",
    )(proj, proj, proj, proj, gb, cw, cb, wqk, ng)


def _merge_kernel(g5_ref, yb_ref, yc_ref, sa_ref, sb_ref, sc_ref, wglu_ref, bglu_ref,
                  wa_ref, wb_ref, wc_ref, o_ref):
    g5 = g5_ref[...]
    ya = g5 * jax.nn.sigmoid(_dot(g5.astype(bf16), wglu_ref[...]) + bglu_ref[...])
    m = sa_ref[...].astype(f32) * _dot(ya.astype(bf16), wa_ref[...])
    m = m + sb_ref[...].astype(f32) * _dot(yb_ref[...], wb_ref[...])
    m = m + sc_ref[...].astype(f32) * _dot(yc_ref[...], wc_ref[...])
    o_ref[...] = m.astype(o_ref.dtype)


def _merge(g5, yb, yc, sg, wglu, bglu, wa, wb, wc, tm):
    t = g5.shape[0]
    d = D_MODEL

    def rows(w):
        return pl.BlockSpec((tm, w), lambda i: (i, 0))

    def full(a):
        return pl.BlockSpec(a.shape, lambda i: (0, 0))

    return pl.pallas_call(
        _merge_kernel,
        grid=(t // tm,),
        in_specs=[
            rows(S5_WIDTH), rows(HG_WIDTH), rows(ML_WP),
            pl.BlockSpec((tm, d), lambda i: (i, 0)),
            pl.BlockSpec((tm, d), lambda i: (i, 1)),
            pl.BlockSpec((tm, d), lambda i: (i, 2)),
            full(wglu), full(bglu), full(wa), full(wb), full(wc),
        ],
        out_specs=rows(d),
        out_shape=jax.ShapeDtypeStruct((t, d), bf16),
        compiler_params=pltpu.CompilerParams(dimension_semantics=("parallel",)),
        name="merge",
    )(g5, yb, yc, sg, sg, sg, wglu, bglu, wa, wb, wc)


def _proj_res_kernel(a_ref, w_ref, r_ref, o_ref):
    o_ref[...] = r_ref[...] + _dot(a_ref[...], w_ref[...])


def _proj_res(a, w, res, tm):
    t, k = a.shape
    n = w.shape[1]
    return pl.pallas_call(
        _proj_res_kernel,
        grid=(t // tm,),
        in_specs=[
            pl.BlockSpec((tm, k), lambda i: (i, 0)),
            pl.BlockSpec((k, n), lambda i: (0, 0)),
            pl.BlockSpec((tm, n), lambda i: (i, 0)),
        ],
        out_specs=pl.BlockSpec((tm, n), lambda i: (i, 0)),
        out_shape=jax.ShapeDtypeStruct((t, n), f32),
        compiler_params=pltpu.CompilerParams(dimension_semantics=("parallel",)),
        name="proj_res",
    )(a, w, res)


def _ffn_kernel(x_ref, g_ref, w_ref, cv_ref, wd_ref, fg_ref, o_ref, hs_ref, buf_ref, car_ref, *, tiles_per_seq, final):
    tm = x_ref.shape[0]
    i = pl.program_id(0)
    j = pl.program_id(1)

    @pl.when(j == 0)
    def _():
        x = x_ref[...]
        hs_ref[...] = _rms(x, g_ref[...]).astype(bf16)
        o_ref[...] = x

    @pl.when(i % tiles_per_seq == 0)
    def _():
        car_ref[j] = jnp.zeros((8, 2 * FFN_TN), f32)

    u = _dot(hs_ref[...], w_ref[...])
    buf_ref[0:8, :] = car_ref[j]
    buf_ref[8:, :] = u
    car_ref[j] = u[tm - 8:, :]
    y = cv_ref[FFN_CONV - 1:FFN_CONV, :] * u + cv_ref[FFN_CONV:FFN_CONV + 1, :]
    for kk in range(FFN_CONV - 1):
        sh = FFN_CONV - 1 - kk
        y = y + cv_ref[kk:kk + 1, :] * buf_ref[pl.ds(8 - sh, tm), :]
    act = (jax.nn.silu(y[:, :FFN_TN]) * y[:, FFN_TN:]).astype(bf16)
    o_ref[...] += _dot(act, wd_ref[...])

    if final:
        @pl.when(j == FFN_NJ - 1)
        def _():
            o_ref[...] = _rms(o_ref[...], fg_ref[...])


def _ffn(x, g, wup, cv, wd, fg, seq, tm, final):
    t, d = x.shape
    return pl.pallas_call(
        functools.partial(_ffn_kernel, tiles_per_seq=seq // tm, final=final),
        grid=(t // tm, FFN_NJ),
        in_specs=[
            pl.BlockSpec((tm, d), lambda i, j: (i, 0), pipeline_mode=pl.Buffered(1)),
            pl.BlockSpec((1, d), lambda i, j: (0, 0)),
            pl.BlockSpec((d, 2 * FFN_TN), lambda i, j: (0, j)),
            pl.BlockSpec((8, 2 * FFN_TN), lambda i, j: (0, j)),
            pl.BlockSpec((FFN_TN, d), lambda i, j: (j, 0)),
            pl.BlockSpec((1, d), lambda i, j: (0, 0)),
        ],
        out_specs=pl.BlockSpec((tm, d), lambda i, j: (i, 0)),
        out_shape=jax.ShapeDtypeStruct((t, d), f32),
        scratch_shapes=[
            pltpu.VMEM((tm, d), bf16),
            pltpu.VMEM((tm + 8, 2 * FFN_TN), f32),
            pltpu.VMEM((FFN_NJ, 8, 2 * FFN_TN), f32),
        ],
        compiler_params=pltpu.CompilerParams(dimension_semantics=("arbitrary", "arbitrary"),
                                             vmem_limit_bytes=VMEM_LIMIT),
        name="ffn",
    )(x, g, wup, cv, wd, fg)


def _pack_ffn(w_up, conv_w, conv_b):
    d = w_up.shape[0]
    wup = w_up.astype(bf16).reshape(d, 2, FFN_NJ, FFN_TN).transpose(0, 2, 1, 3).reshape(d, 2 * FFN_DIM)
    cv = jnp.concatenate([conv_w, conv_b[None, :], jnp.zeros((8 - FFN_CONV - 1, 2 * FFN_DIM), f32)], axis=0)
    cv = cv.reshape(8, 2, FFN_NJ, FFN_TN).transpose(0, 2, 1, 3).reshape(8, 2 * FFN_DIM)
    return wup, cv


def _pad_heads(w, axis):
    shp = w.shape
    w = w.reshape(shp[:axis] + (ML_HEADS, ML_DH) + shp[axis + 1:])
    pad = [(0, 0)] * w.ndim
    pad[axis + 1] = (0, ML_DP - ML_DH)
    w = jnp.pad(w, pad)
    return w.reshape(shp[:axis] + (ML_WP,) + shp[axis + 1:])


def _pack_w_in(w):
    d = w.shape[0]
    w = w.astype(bf16)
    parts = [
        w[:, 0:3584],
        _pad_heads(w[:, 3584:4352], 1),
        _pad_heads(w[:, 4352:5120], 1),
        _pad_heads(w[:, 5120:5888], 1),
        w[:, 5888:5896],
        jnp.zeros((d, MIXER_COLS - OFF_IGFG - 2 * ML_HEADS), w.dtype),
    ]
    return jnp.concatenate(parts, axis=1), w[:, 5896:]


def _layer(x, p, nb, seq, tiles, last):
    proj = _norm_proj(x, p["mix_norm"], p["w_mixer"], act="none", out_dtype=f32,
                      tm=tiles["tm_proj"], tn=tiles["tn_mixer"])
    sg = _norm_proj(x, p["mix_norm"], p["w_gates"], act="sigmoid", out_dtype=bf16,
                    tm=tiles["tm_proj"], tn=tiles["tn_gates"])
    g5 = _s5_core(proj, p["s5_m"], p["s5_bst"], p["s5_cst"], p["s5_av"], nb)
    yb = _hgrn(proj, p["hg_lb"], p["hg_norm"], nb, seq, tiles["tb_hgrn"])
    yc = _mlstm(proj, p["ml_gb"], p["ml_cw"], p["ml_cb"], p["ml_wqk"], p["ml_norm"], nb, seq, tiles["tb"])
    merged = _merge(g5, yb, yc, sg, p["s5_wglu"], p["s5_bglu"], p["wb_a"], p["wb_b"], p["wb_c"], tiles["tm_merge"])
    x = _proj_res(merged, p["w_out"], x, tiles["tm_merge"])
    return _ffn(x, p["ffn_norm"], p["ffn_wup"], p["ffn_cv"], p["ffn_wd"], p["final_norm"], seq, tiles["tm_ffn"], last)


def _tiles(seq):
    return {
        "tm_proj": min(1024, seq), "tn_mixer": MIXER_COLS // 4, "tn_gates": GATE_COLS // 3,
        "tb_hgrn": min(1024, seq), "tb": min(512, seq), "tm_merge": min(512, seq), "tm_ffn": min(1024, seq),
    }


def kernel(x, mix_norm, w_in, s5_lam_re, s5_lam_im, s5_log_dt, s5_b_re, s5_b_im, s5_c_re, s5_c_im, s5_d, s5_w_glu, s5_b_glu, hg_lower_bounds, hg_norm, ml_conv_w, ml_conv_b, ml_w_qk, ml_b_ig, ml_b_fg, ml_norm, w_branch, w_out, ffn_norm, ffn_w_up, ffn_conv_w, ffn_conv_b, ffn_w_down, final_norm):
    nb, seq, d = x.shape
    depth = w_in.shape[0]
    tiles = _tiles(seq)
    lbs = jax.nn.softmax(hg_lower_bounds.astype(f32), axis=0)
    lbs = jnp.cumsum(lbs, axis=0) - lbs[0:1]
    xs = x.astype(f32).reshape(nb * seq, d)
    for l in range(depth):
        w_mixer, w_gates = _pack_w_in(w_in[l])
        s5_m, s5_bst, s5_cst, s5_av = _s5_prep(
            s5_lam_re[l], s5_lam_im[l], s5_log_dt[l], s5_b_re[l], s5_b_im[l], s5_c_re[l], s5_c_im[l], s5_d[l])
        wqk = ml_w_qk[l]
        wq = jnp.pad(wqk[:, :, :ML_DH], ((0, 0), (0, ML_DP - ML_DH), (0, ML_DP - ML_DH)))
        wk = jnp.pad(wqk[:, :, ML_DH:], ((0, 0), (0, ML_DP - ML_DH), (0, ML_DP - ML_DH)))
        gbias = jnp.concatenate([ml_b_ig[l], ml_b_fg[l], jnp.zeros((128 - 2 * ML_HEADS,), f32)])[None, :]
        wb = w_branch[l]
        ffn_wup, ffn_cv = _pack_ffn(ffn_w_up[l], ffn_conv_w[l], ffn_conv_b[l])
        p = {
            "mix_norm": mix_norm[l][None, :], "w_mixer": w_mixer, "w_gates": w_gates,
            "s5_m": s5_m, "s5_bst": s5_bst, "s5_cst": s5_cst, "s5_av": s5_av,
            "s5_wglu": s5_w_glu[l].astype(bf16), "s5_bglu": s5_b_glu[l][None, :],
            "hg_lb": lbs[l][None, :], "hg_norm": hg_norm[l][None, :],
            "ml_gb": gbias, "ml_cw": _pad_heads(ml_conv_w[l], 1), "ml_cb": _pad_heads(ml_conv_b[l][None, :], 1),
            "ml_wqk": jnp.concatenate([wq, wk], axis=-1).astype(bf16),
            "ml_norm": _pad_heads(ml_norm[l][None, :], 1),
            "wb_a": wb[:S5_WIDTH].astype(bf16), "wb_b": wb[S5_WIDTH:S5_WIDTH + HG_WIDTH].astype(bf16),
            "wb_c": _pad_heads(wb[S5_WIDTH + HG_WIDTH:], 0).astype(bf16),
            "w_out": w_out[l].astype(bf16),
            "ffn_norm": ffn_norm[l][None, :], "ffn_wup": ffn_wup, "ffn_cv": ffn_cv, "ffn_wd": ffn_w_down[l].astype(bf16),
            "final_norm": final_norm[None, :],
        }
        xs = _layer(xs, p, nb, seq, tiles, l == depth - 1)
    return xs.reshape(nb, seq, d).astype(x.dtype)
```

```python
import functools

import jax
import jax.numpy as jnp
from jax import lax
from jax.experimental import pallas as pl
from jax.experimental.pallas import tpu as pltpu

f32 = jnp.float32
bf16 = jnp.bfloat16

D_MODEL = 2048
EPS = 1e-6
CHUNK = 64

S5_GROUPS = 32
S5_P = 16
S5_WIDTH = 512
S5_STATE = 64
S5_L = 8
S5_OCT = 8

HG_HEADS = 6
HG_D = 128
HG_WIDTH = 768
HG_SUB = 8
HG_PER = 3

ML_HEADS = 4
ML_DH = 192
ML_DP = 256
ML_WIDTH = 768
ML_WP = ML_HEADS * ML_DP
ML_CONV = 4
ML_PER = 2

FFN_DIM = 5632
FFN_CONV = 3
FFN_TN = 512
FFN_NJ = FFN_DIM // FFN_TN

OFF_U, OFF_Q, OFF_F, OFF_I, OFF_OG = 0, 512, 1280, 2048, 2816
OFF_CX, OFF_V, OFF_O, OFF_IGFG = 3584, 4608, 5632, 6656
MIXER_COLS = 7168
GATE_COLS = 3 * D_MODEL

VMEM_LIMIT = 60 * 1024 * 1024


def _split3(x):
    hi = x.astype(bf16)
    r1 = x - hi.astype(f32)
    mid = r1.astype(bf16)
    lo = (r1 - mid.astype(f32)).astype(bf16)
    return hi, mid, lo


def _dot(a, b):
    return jnp.dot(a, b, preferred_element_type=f32)


def _dot_nt(a, b):
    return lax.dot_general(a, b, (((1,), (1,)), ((), ())), preferred_element_type=f32)


def _dot_tn(a, b):
    return lax.dot_general(a, b, (((0,), (0,)), ((), ())), preferred_element_type=f32)


def _exact_dot(a_bf16, x):
    hi, mid, lo = _split3(x)
    return _dot(a_bf16, hi) + _dot(a_bf16, mid) + _dot(a_bf16, lo)


def _rms(x, g):
    ms = jnp.mean(x * x, axis=-1, keepdims=True)
    return x * lax.rsqrt(ms + EPS) * g


def _norm_proj_kernel(x_ref, g_ref, w_ref, o_ref, hs_ref, *, act):
    @pl.when(pl.program_id(1) == 0)
    def _():
        hs_ref[...] = _rms(x_ref[...], g_ref[...]).astype(bf16)

    y = _dot(hs_ref[...], w_ref[...])
    if act == "sigmoid":
        y = jax.nn.sigmoid(y)
    o_ref[...] = y.astype(o_ref.dtype)


def _norm_proj(x, g, w, *, act, out_dtype, tm, tn):
    t, d = x.shape
    n = w.shape[1]
    return pl.pallas_call(
        functools.partial(_norm_proj_kernel, act=act),
        grid=(t // tm, n // tn),
        in_specs=[
            pl.BlockSpec((tm, d), lambda i, j: (i, 0)),
            pl.BlockSpec((1, d), lambda i, j: (0, 0)),
            pl.BlockSpec((d, tn), lambda i, j: (0, j)),
        ],
        out_specs=pl.BlockSpec((tm, tn), lambda i, j: (i, j)),
        out_shape=jax.ShapeDtypeStruct((t, n), out_dtype),
        scratch_shapes=[pltpu.VMEM((tm, d), bf16)],
        compiler_params=pltpu.CompilerParams(dimension_semantics=("parallel", "arbitrary"),
                                             vmem_limit_bytes=VMEM_LIMIT),
        name="norm_proj_" + act,
    )(x, g, w)


def _s5_kernel(u_ref, m_ref, bst_ref, cst_ref, av_ref, y_ref, sin_ref, xp_ref):
    nc = u_ref.shape[0] // S5_L
    half = S5_OCT * S5_STATE
    xb = jnp.concatenate([u_ref[pl.ds(t, nc, stride=S5_L), :] for t in range(S5_L)], axis=1).astype(bf16)
    sin_ref[...] = _dot(xb, bst_ref[0])
    ar = av_ref[0, 0:1, :]
    ai = av_ref[0, 1:2, :]

    def step(c, carry):
        xr, xi = carry
        xp_ref[pl.ds(c, 1), :] = jnp.concatenate([xr, xi], axis=1)
        s = sin_ref[pl.ds(c, 1), :]
        return ar * xr - ai * xi + s[:, :half], ar * xi + ai * xr + s[:, half:]

    z = jnp.zeros((1, half), f32)
    lax.fori_loop(0, nc, step, (z, z), unroll=8)
    y = jax.nn.gelu(_dot(xb, m_ref[0]) + _dot(xp_ref[...].astype(bf16), cst_ref[0]))
    for t in range(S5_L):
        y_ref[pl.ds(t, nc, stride=S5_L), :] = y[:, t * 128:(t + 1) * 128]


def _s5_core(proj, m, bst, cst, av, nb):
    t = proj.shape[0]
    seq = t // nb
    nc = seq // S5_L
    noct = S5_GROUPS // S5_OCT
    cols = S5_L * 128
    st = 2 * S5_OCT * S5_STATE
    return pl.pallas_call(
        _s5_kernel,
        grid=(noct, nb),
        in_specs=[
            pl.BlockSpec((seq, 128), lambda o, b: (b, OFF_U // 128 + o)),
            pl.BlockSpec((1, cols, cols), lambda o, b: (o, 0, 0)),
            pl.BlockSpec((1, cols, st), lambda o, b: (o, 0, 0)),
            pl.BlockSpec((1, st, cols), lambda o, b: (o, 0, 0)),
            pl.BlockSpec((1, 8, st // 2), lambda o, b: (o, 0, 0)),
        ],
        out_specs=pl.BlockSpec((seq, 128), lambda o, b: (b, o)),
        out_shape=jax.ShapeDtypeStruct((t, S5_WIDTH), f32),
        scratch_shapes=[pltpu.VMEM((nc, st), f32), pltpu.VMEM((nc, st), f32)],
        compiler_params=pltpu.CompilerParams(dimension_semantics=("parallel", "parallel")),
        name="s5_core",
    )(proj, m, bst, cst, av)


def _s5_prep(lam_re, lam_im, log_dt, b_re, b_im, c_re, c_im, d_skip):
    hp = lax.Precision.HIGHEST
    g, n, p, ln, oc = S5_GROUPS, S5_STATE, S5_P, S5_L, S5_OCT
    no = g // oc
    dt = jnp.exp(log_dt)[:, None]
    lr, li = lam_re, lam_im
    mag = jnp.exp(lr * dt)
    ar, ai = mag * jnp.cos(li * dt), mag * jnp.sin(li * dt)
    den = lr * lr + li * li
    cr = ((ar - 1.0) * lr + ai * li) / den
    ci = (ai * lr - (ar - 1.0) * li) / den
    bbr = cr[..., None] * b_re - ci[..., None] * b_im
    bbi = cr[..., None] * b_im + ci[..., None] * b_re
    tau = jnp.arange(ln + 1, dtype=f32)[:, None, None]
    pmag = jnp.exp(tau * (lr * dt)[None])
    pr, pi = pmag * jnp.cos(tau * (li * dt)[None]), pmag * jnp.sin(tau * (li * dt)[None])
    cpr = c_re[None] * pr[:, :, None, :] - c_im[None] * pi[:, :, None, :]
    cpi = c_re[None] * pi[:, :, None, :] + c_im[None] * pr[:, :, None, :]
    kern = (jnp.einsum("tgpn,gnq->tgpq", cpr[:ln], bbr, precision=hp)
            - jnp.einsum("tgpn,gnq->tgpq", cpi[:ln], bbi, precision=hp))
    dskip = d_skip.reshape(g, p)
    kern = kern.at[0].add(jnp.eye(p, dtype=f32)[None] * dskip[:, :, None])
    kern = jnp.concatenate([kern, jnp.zeros((1, g, p, p), f32)], axis=0)
    eye = jnp.eye(oc, dtype=f32)
    blk = kern.reshape(ln + 1, no, oc, p, p).transpose(0, 1, 2, 4, 3)
    blk = (blk[:, :, :, :, None, :] * eye[None, None, :, None, :, None]).reshape(ln + 1, no, oc * p, oc * p)
    s_idx = jnp.arange(ln)[:, None]
    t_idx = jnp.arange(ln)[None, :]
    lag = jnp.where(t_idx >= s_idx, t_idx - s_idx, ln)
    m8 = blk[lag].transpose(2, 0, 3, 1, 4).reshape(no, ln * oc * p, ln * oc * p)
    tau_b = (ln - 1) - tau[:ln]
    bmag = jnp.exp(tau_b * (lr * dt)[None])
    prs, pis = bmag * jnp.cos(tau_b * (li * dt)[None]), bmag * jnp.sin(tau_b * (li * dt)[None])
    bbrt, bbit = bbr.transpose(0, 2, 1), bbi.transpose(0, 2, 1)
    bst_r = prs[:, :, None, :] * bbrt[None] - pis[:, :, None, :] * bbit[None]
    bst_i = prs[:, :, None, :] * bbit[None] + pis[:, :, None, :] * bbrt[None]

    def expand_b(x):
        x = x.reshape(ln, no, oc, p, n)
        return (x[:, :, :, :, None, :] * eye[None, None, :, None, :, None]).reshape(ln, no, oc * p, oc * n)

    bst8 = jnp.concatenate([expand_b(bst_r), expand_b(bst_i)], axis=-1)
    bst8 = bst8.transpose(1, 0, 2, 3).reshape(no, ln * oc * p, 2 * oc * n)
    cst = jnp.stack([cpr[1:], -cpi[1:]], axis=0)
    cst = cst.reshape(2, ln, no, oc, p, n).transpose(0, 1, 2, 3, 5, 4)
    cst8 = cst[:, :, :, :, :, None, :] * eye[None, None, None, :, None, :, None]
    cst8 = cst8.reshape(2, ln, no, oc * n, oc * p).transpose(2, 0, 3, 1, 4).reshape(no, 2 * oc * n, ln * oc * p)
    av = jnp.stack([pr[ln].reshape(no, oc * n), pi[ln].reshape(no, oc * n)], axis=1)
    av = jnp.concatenate([av, jnp.zeros((no, 6, oc * n), f32)], axis=1)
    return m8.astype(bf16), bst8.astype(bf16), cst8.astype(bf16), av


def _hg_consts():
    row = lax.broadcasted_iota(jnp.int32, (CHUNK, CHUNK), 0)
    col = lax.broadcasted_iota(jnp.int32, (CHUNK, CHUNK), 1)
    diag_mask = ((row // HG_SUB) == (col // HG_SUB)) & (col <= row)
    levels = []
    half = HG_SUB
    while half < CHUNK:
        levels.append((half, ((row // (2 * half)) == (col // (2 * half)))
                       & ((row & half) != 0) & ((col & half) == 0)))
        half *= 2
    wrow = lax.broadcasted_iota(jnp.int32, (HG_SUB * HG_D, CHUNK), 0)
    wcol = lax.broadcasted_iota(jnp.int32, (HG_SUB * HG_D, CHUNK), 1)
    wsel = ((wcol % HG_SUB) == (wrow // HG_D)).astype(bf16)
    return diag_mask, levels, wsel


def _hg_prelude(f_ref, lb, tril, k_s, cum_s):
    tb = f_ref.shape[0]
    f = lb + (1.0 - lb) * jax.nn.sigmoid(f_ref[...])
    k_s[...] = 1.0 - f
    logf = jnp.log2(f)
    for c in range(tb // CHUNK):
        cum_s[c * CHUNK:(c + 1) * CHUNK, :] = _exact_dot(tril, logf[c * CHUNK:(c + 1) * CHUNK])


def _hg_chunk(c, q_ref, i_ref, og_ref, ng, y_ref, lanes, k_s, cum_s, st, consts):
    diag_mask, levels, wsel = consts
    nblk = CHUNK // HG_SUB
    r0 = c * CHUNK
    q = jax.nn.silu(q_ref[r0:r0 + CHUNK, :])
    k = k_s[r0:r0 + CHUNK, :]
    cum = cum_s[r0:r0 + CHUNK, :]
    vb = i_ref[r0:r0 + CHUNK, :].astype(bf16)
    last = cum_s[r0 + CHUNK - 1:r0 + CHUNK, :]
    inter = _dot_nt((q * jnp.exp2(cum)).astype(bf16), st.astype(bf16))
    slabs = []
    for s in range(HG_SUB):
        pieces = []
        for b in range(nblk):
            rr = r0 + b * HG_SUB + s
            blk = slice(b * HG_SUB, (b + 1) * HG_SUB)
            dec = jnp.exp2(jnp.minimum(cum[blk] - cum_s[rr:rr + 1, :], 0.0))
            pieces.append((q[blk] * k_s[rr:rr + 1, :]) * dec)
        slabs.append(jnp.concatenate(pieces, axis=0).astype(bf16))
    sc = jnp.where(diag_mask, _dot(jnp.concatenate(slabs, axis=1), wsel), 0.0)
    for half, mask in levels:
        zs = []
        for base in range(0, CHUNK, 2 * half):
            mid = cum_s[r0 + base + half - 1:r0 + base + half, :]
            lo, hi = slice(base, base + half), slice(base + half, base + 2 * half)
            zs.append(k[lo] * jnp.exp2(mid - cum[lo]))
            zs.append(q[hi] * jnp.exp2(cum[hi] - mid))
        z = jnp.concatenate(zs, axis=0).astype(bf16)
        sc = jnp.where(mask, _dot_nt(z, z), sc)
    out = inter + _dot(sc.astype(bf16), vb)
    kdec = (k * jnp.exp2(last - cum)).astype(bf16)
    st = st * jnp.exp2(last) + _dot_tn(vb, kdec)
    o = _rms(out, ng) * jax.nn.silu(og_ref[r0:r0 + CHUNK, :])
    y_ref[r0:r0 + CHUNK, lanes] = o.astype(y_ref.dtype)
    return st


def _ml_prelude(cx_ref, lanes, gt_ref, gb, head, cw, cb, wqk, tril, carry_ref, buf_ref, q_s, k_s):
    tb = gt_ref.shape[0]
    nch = tb // CHUNK
    lane = lax.broadcasted_iota(jnp.int32, (CHUNK, 128), 1)
    lane_b = lax.broadcasted_iota(jnp.int32, (tb, 128), 1)
    gts = gt_ref[...] + gb
    ib_all = jnp.sum(jnp.where(lane_b == head, gts, 0.0), axis=-1, keepdims=True)
    lf_all = jax.nn.log_sigmoid(jnp.sum(jnp.where(lane_b == head + ML_HEADS, gts, 0.0), axis=-1, keepdims=True))
    lfc = jnp.zeros((CHUNK, 128), f32)
    ibc = jnp.zeros((CHUNK, 128), f32)
    for c in range(nch):
        lfc = jnp.where(lane == c, lf_all[c * CHUNK:(c + 1) * CHUNK], lfc)
        ibc = jnp.where(lane == c, ib_all[c * CHUNK:(c + 1) * CHUNK], ibc)
    bc = _exact_dot(tril, lfc)
    zt = (ibc - bc).T
    cx = cx_ref[:, lanes]
    buf_ref[0:8, :] = carry_ref[...]
    buf_ref[8:, :] = cx
    carry_ref[...] = cx[tb - 8:, :]
    ca = cw[ML_CONV - 1:ML_CONV, :] * cx + cb
    for kk in range(ML_CONV - 1):
        sh = ML_CONV - 1 - kk
        ca = ca + cw[kk:kk + 1, :] * buf_ref[pl.ds(8 - sh, tb), :]
    qk = _dot(jax.nn.silu(ca).astype(bf16), wqk)
    q_s[...] = qk[:, :ML_DP].astype(bf16)
    k_s[...] = (qk[:, ML_DP:] * (ML_DH ** -0.5)).astype(bf16)
    return ib_all, bc, zt


def _ml_chunk(c, gates, v_ref, o_ref, lanes, ng, y_ref, q_s, k_s, cmat, m_prev):
    ib_all, bc, zt = gates
    row = lax.broadcasted_iota(jnp.int32, (CHUNK, CHUNK), 0)
    col = lax.broadcasted_iota(jnp.int32, (CHUNK, CHUNK), 1)
    lane_w = lax.broadcasted_iota(jnp.int32, (CHUNK, ML_DP), 1)
    r0 = c * CHUNK
    ib = ib_all[r0:r0 + CHUNK]
    bcum = bc[:, c:c + 1]
    dmat = jnp.where(row >= col, bcum + zt[c:c + 1, :], -jnp.inf)
    m_t = jnp.maximum(bcum + m_prev, jnp.max(dmat, axis=-1, keepdims=True))
    m_new = m_t[CHUNK - 1:CHUNK, :]
    b_last = bcum[CHUNK - 1:CHUNK, :]
    q = q_s[r0:r0 + CHUNK, :]
    k = k_s[r0:r0 + CHUNK, :]
    vaug = jnp.where(lane_w == ML_DH, 1.0, v_ref[r0:r0 + CHUNK, lanes]).astype(bf16)
    w_inter = jnp.exp(bcum + m_prev - m_t)
    w_intra = jnp.exp(dmat - m_t) * _dot_nt(q, k)
    num = w_inter * _dot(q, cmat.astype(bf16)) + _dot(w_intra.astype(bf16), vaug)
    decay = jnp.exp(b_last + m_prev - m_new)
    ws = jnp.exp(b_last - bcum + ib - m_new)
    cmat = decay * cmat + _dot_tn((ws * k.astype(f32)).astype(bf16), vaug)
    den = jnp.sum(jnp.where(lane_w == ML_DH, num, 0.0), axis=-1, keepdims=True)
    h = num / jnp.maximum(jnp.abs(den), jnp.exp(-m_t))
    hm = jnp.where(lane_w < ML_DH, h, 0.0)
    ms = jnp.sum(hm * hm, axis=-1, keepdims=True) * (1.0 / ML_DH)
    hn = hm * lax.rsqrt(ms + EPS) * ng
    y_ref[r0:r0 + CHUNK, lanes] = (hn * jax.nn.sigmoid(o_ref[r0:r0 + CHUNK, lanes])).astype(y_ref.dtype)
    return cmat, m_new


def _mixers_kernel(*refs):
    hg_in = refs[:4 * HG_PER]
    lb_ref, hng_ref, cx_ref, v_ref, o_ref, gt_ref, gb_ref, cw_ref, cb_ref, wqk_ref, mng_ref = refs[4 * HG_PER:4 * HG_PER + 11]
    yb_ref, yc_ref = refs[4 * HG_PER + 11:4 * HG_PER + 13]
    st_ref, hk_s, cum_s, c_ref, m_ref, carry_ref, buf_ref, q_s, k_s = refs[4 * HG_PER + 13:]
    tb = gt_ref.shape[0]
    group = pl.program_id(1)

    @pl.when(pl.program_id(2) == 0)
    def _():
        st_ref[...] = jnp.zeros_like(st_ref)
        c_ref[...] = jnp.zeros_like(c_ref)
        m_ref[...] = jnp.zeros_like(m_ref)
        carry_ref[...] = jnp.zeros_like(carry_ref)

    row = lax.broadcasted_iota(jnp.int32, (CHUNK, CHUNK), 0)
    col = lax.broadcasted_iota(jnp.int32, (CHUNK, CHUNK), 1)
    tril = (row >= col).astype(bf16)
    consts = _hg_consts()
    gb = gb_ref[...]
    hg_lanes = [slice(h * HG_D, (h + 1) * HG_D) for h in range(HG_PER)]
    ml_lanes = [slice(h * ML_DP, (h + 1) * ML_DP) for h in range(ML_PER)]

    for h in range(HG_PER):
        _hg_prelude(hg_in[4 * h + 1], lb_ref[:, hg_lanes[h]], tril, hk_s.at[h], cum_s.at[h])
    gates = []
    for h in range(ML_PER):
        gates.append(_ml_prelude(cx_ref, ml_lanes[h], gt_ref, gb, group * ML_PER + h, cw_ref[:, ml_lanes[h]],
                                 cb_ref[:, ml_lanes[h]], wqk_ref[h], tril, carry_ref.at[h], buf_ref.at[h],
                                 q_s.at[h], k_s.at[h]))

    sts = [st_ref[h] for h in range(HG_PER)]
    cms = [c_ref[h] for h in range(ML_PER)]
    mps = [m_ref[h][0:1, 0:1] for h in range(ML_PER)]
    for c in range(tb // CHUNK):
        for h in range(max(HG_PER, ML_PER)):
            if h < HG_PER:
                sts[h] = _hg_chunk(c, hg_in[4 * h], hg_in[4 * h + 2], hg_in[4 * h + 3], hng_ref[:, hg_lanes[h]], yb_ref,
                                   hg_lanes[h], hk_s.at[h], cum_s.at[h], sts[h], consts)
            if h < ML_PER:
                cms[h], mps[h] = _ml_chunk(c, gates[h], v_ref, o_ref, ml_lanes[h], mng_ref[:, ml_lanes[h]], yc_ref,
                                           q_s.at[h], k_s.at[h], cms[h], mps[h])
    for h in range(HG_PER):
        st_ref[h] = sts[h]
    for h in range(ML_PER):
        c_ref[h] = cms[h]
        m_ref[h] = jnp.broadcast_to(mps[h], m_ref.shape[1:])


def _mixers(proj, lb, hng, gb, cw, cb, wqk, mng, nb, seq, tb):
    t = proj.shape[0]
    nj = seq // tb
    hg_w, ml_w = HG_PER * HG_D, ML_PER * ML_DP

    def hseg(off, h):
        return pl.BlockSpec((tb, HG_D), lambda b, g, j, o=off // HG_D + h: (b * nj + j, o + HG_PER * g))

    def mseg(off):
        return pl.BlockSpec((tb, ml_w), lambda b, g, j, o=off // ml_w: (b * nj + j, o + g))

    in_specs = []
    for h in range(HG_PER):
        in_specs += [hseg(OFF_Q, h), hseg(OFF_F, h), hseg(OFF_I, h), hseg(OFF_OG, h)]
    in_specs += [
        pl.BlockSpec((1, hg_w), lambda b, g, j: (0, g)),
        pl.BlockSpec((1, hg_w), lambda b, g, j: (0, g)),
        mseg(OFF_CX), mseg(OFF_V), mseg(OFF_O),
        pl.BlockSpec((tb, 128), lambda b, g, j: (b * nj + j, OFF_IGFG // 128)),
        pl.BlockSpec((1, 128), lambda b, g, j: (0, 0)),
        pl.BlockSpec((ML_CONV, ml_w), lambda b, g, j: (0, g)),
        pl.BlockSpec((1, ml_w), lambda b, g, j: (0, g)),
        pl.BlockSpec((ML_PER, ML_DP, 2 * ML_DP), lambda b, g, j: (g, 0, 0)),
        pl.BlockSpec((1, ml_w), lambda b, g, j: (0, g)),
    ]
    return pl.pallas_call(
        _mixers_kernel,
        grid=(nb, HG_HEADS // HG_PER, nj),
        in_specs=in_specs,
        out_specs=[pl.BlockSpec((tb, hg_w), lambda b, g, j: (b * nj + j, g)),
                   pl.BlockSpec((tb, ml_w), lambda b, g, j: (b * nj + j, g))],
        out_shape=[jax.ShapeDtypeStruct((t, HG_WIDTH), bf16), jax.ShapeDtypeStruct((t, ML_WP), bf16)],
        scratch_shapes=[
            pltpu.VMEM((HG_PER, HG_D, HG_D), f32),
            pltpu.VMEM((HG_PER, tb, HG_D), f32),
            pltpu.VMEM((HG_PER, tb, HG_D), f32),
            pltpu.VMEM((ML_PER, ML_DP, ML_DP), f32),
            pltpu.VMEM((ML_PER, 8, 128), f32),
            pltpu.VMEM((ML_PER, 8, ML_DP), f32),
            pltpu.VMEM((ML_PER, tb + 8, ML_DP), f32),
            pltpu.VMEM((ML_PER, tb, ML_DP), bf16),
            pltpu.VMEM((ML_PER, tb, ML_DP), bf16),
        ],
        compiler_params=pltpu.CompilerParams(dimension_semantics=("parallel", "parallel", "arbitrary")),
        name="mixers",
    )(*([proj] * (4 * HG_PER)), lb, hng, proj, proj, proj, proj, gb, cw, cb, wqk, mng)


def _merge_kernel(g5_ref, yb_ref, yc_ref, sa_ref, sb_ref, sc_ref, wglu_ref, bglu_ref,
                  wa_ref, wb_ref, wc_ref, o_ref):
    g5 = g5_ref[...]
    ya = g5 * jax.nn.sigmoid(_dot(g5.astype(bf16), wglu_ref[...]) + bglu_ref[...])
    m = sa_ref[...].astype(f32) * _dot(ya.astype(bf16), wa_ref[...])
    m = m + sb_ref[...].astype(f32) * _dot(yb_ref[...], wb_ref[...])
    m = m + sc_ref[...].astype(f32) * _dot(yc_ref[...], wc_ref[...])
    o_ref[...] = m.astype(o_ref.dtype)


def _merge(g5, yb, yc, sg, wglu, bglu, wa, wb, wc, tm):
    t = g5.shape[0]
    d = D_MODEL

    def rows(w):
        return pl.BlockSpec((tm, w), lambda i: (i, 0))

    def full(a):
        return pl.BlockSpec(a.shape, lambda i: (0, 0))

    return pl.pallas_call(
        _merge_kernel,
        grid=(t // tm,),
        in_specs=[
            rows(S5_WIDTH), rows(HG_WIDTH), rows(ML_WP),
            pl.BlockSpec((tm, d), lambda i: (i, 0)),
            pl.BlockSpec((tm, d), lambda i: (i, 1)),
            pl.BlockSpec((tm, d), lambda i: (i, 2)),
            full(wglu), full(bglu), full(wa), full(wb), full(wc),
        ],
        out_specs=rows(d),
        out_shape=jax.ShapeDtypeStruct((t, d), bf16),
        compiler_params=pltpu.CompilerParams(dimension_semantics=("parallel",)),
        name="merge",
    )(g5, yb, yc, sg, sg, sg, wglu, bglu, wa, wb, wc)


def _proj_res_kernel(a_ref, w_ref, r_ref, o_ref):
    o_ref[...] = r_ref[...] + _dot(a_ref[...], w_ref[...])


def _proj_res(a, w, res, tm):
    t, k = a.shape
    n = w.shape[1]
    return pl.pallas_call(
        _proj_res_kernel,
        grid=(t // tm,),
        in_specs=[
            pl.BlockSpec((tm, k), lambda i: (i, 0)),
            pl.BlockSpec((k, n), lambda i: (0, 0)),
            pl.BlockSpec((tm, n), lambda i: (i, 0)),
        ],
        out_specs=pl.BlockSpec((tm, n), lambda i: (i, 0)),
        out_shape=jax.ShapeDtypeStruct((t, n), f32),
        compiler_params=pltpu.CompilerParams(dimension_semantics=("parallel",)),
        name="proj_res",
    )(a, w, res)


def _ffn_kernel(x_ref, g_ref, wa_ref, wb_ref, cva_ref, cvb_ref, wd_ref, fg_ref, o_ref,
                hs_ref, bufa_ref, bufb_ref, cara_ref, carb_ref, *, tiles_per_seq, final):
    tm = x_ref.shape[0]
    i = pl.program_id(0)
    j = pl.program_id(1)

    @pl.when(j == 0)
    def _():
        x = x_ref[...]
        hs_ref[...] = _rms(x, g_ref[...]).astype(bf16)
        o_ref[...] = x

    @pl.when(i % tiles_per_seq == 0)
    def _():
        cara_ref[j] = jnp.zeros((8, FFN_TN), f32)
        carb_ref[j] = jnp.zeros((8, FFN_TN), f32)

    hs = hs_ref[...]

    def conv(w_ref, cv_ref, buf_ref, car_ref):
        u = _dot(hs, w_ref[...])
        buf_ref[0:8, :] = car_ref[j]
        buf_ref[8:, :] = u
        car_ref[j] = u[tm - 8:, :]
        y = cv_ref[FFN_CONV - 1:FFN_CONV, :] * u + cv_ref[FFN_CONV:FFN_CONV + 1, :]
        for kk in range(FFN_CONV - 1):
            sh = FFN_CONV - 1 - kk
            y = y + cv_ref[kk:kk + 1, :] * buf_ref[pl.ds(8 - sh, tm), :]
        return y

    a = conv(wa_ref, cva_ref, bufa_ref, cara_ref)
    b = conv(wb_ref, cvb_ref, bufb_ref, carb_ref)
    act = (jax.nn.silu(a) * b).astype(bf16)
    o_ref[...] += _dot(act, wd_ref[...])

    if final:
        @pl.when(j == FFN_NJ - 1)
        def _():
            o_ref[...] = _rms(o_ref[...], fg_ref[...])


def _ffn(x, g, wup, cv, wd, fg, seq, tm, final):
    t, d = x.shape
    return pl.pallas_call(
        functools.partial(_ffn_kernel, tiles_per_seq=seq // tm, final=final),
        grid=(t // tm, FFN_NJ),
        in_specs=[
            pl.BlockSpec((tm, d), lambda i, j: (i, 0), pipeline_mode=pl.Buffered(1)),
            pl.BlockSpec((1, d), lambda i, j: (0, 0)),
            pl.BlockSpec((d, FFN_TN), lambda i, j: (0, j)),
            pl.BlockSpec((d, FFN_TN), lambda i, j: (0, j + FFN_NJ)),
            pl.BlockSpec((8, FFN_TN), lambda i, j: (0, j)),
            pl.BlockSpec((8, FFN_TN), lambda i, j: (0, j + FFN_NJ)),
            pl.BlockSpec((FFN_TN, d), lambda i, j: (j, 0)),
            pl.BlockSpec((1, d), lambda i, j: (0, 0)),
        ],
        out_specs=pl.BlockSpec((tm, d), lambda i, j: (i, 0)),
        out_shape=jax.ShapeDtypeStruct((t, d), f32),
        scratch_shapes=[
            pltpu.VMEM((tm, d), bf16),
            pltpu.VMEM((tm + 8, FFN_TN), f32),
            pltpu.VMEM((tm + 8, FFN_TN), f32),
            pltpu.VMEM((FFN_NJ, 8, FFN_TN), f32),
            pltpu.VMEM((FFN_NJ, 8, FFN_TN), f32),
        ],
        compiler_params=pltpu.CompilerParams(dimension_semantics=("arbitrary", "arbitrary"),
                                             vmem_limit_bytes=VMEM_LIMIT),
        name="ffn",
    )(x, g, wup, wup, cv, cv, wd, fg)


def _pad_heads(w, axis):
    shp = w.shape
    w = w.reshape(shp[:axis] + (ML_HEADS, ML_DH) + shp[axis + 1:])
    pad = [(0, 0)] * w.ndim
    pad[axis + 1] = (0, ML_DP - ML_DH)
    w = jnp.pad(w, pad)
    return w.reshape(shp[:axis] + (ML_WP,) + shp[axis + 1:])


def _pack_w_in(w):
    d = w.shape[0]
    w = w.astype(bf16)
    parts = [
        w[:, 0:3584],
        _pad_heads(w[:, 3584:4352], 1),
        _pad_heads(w[:, 4352:5120], 1),
        _pad_heads(w[:, 5120:5888], 1),
        w[:, 5888:5896],
        jnp.zeros((d, MIXER_COLS - OFF_IGFG - 2 * ML_HEADS), w.dtype),
    ]
    return jnp.concatenate(parts, axis=1), w[:, 5896:]


def _layer(x, p, nb, seq, tiles, last):
    proj = _norm_proj(x, p["mix_norm"], p["w_mixer"], act="none", out_dtype=f32,
                      tm=tiles["tm_proj"], tn=tiles["tn_mixer"])
    sg = _norm_proj(x, p["mix_norm"], p["w_gates"], act="sigmoid", out_dtype=bf16,
                    tm=tiles["tm_proj"], tn=tiles["tn_gates"])
    g5 = _s5_core(proj, p["s5_m"], p["s5_bst"], p["s5_cst"], p["s5_av"], nb)
    yb, yc = _mixers(proj, p["hg_lb"], p["hg_norm"], p["ml_gb"], p["ml_cw"], p["ml_cb"], p["ml_wqk"], p["ml_norm"],
                     nb, seq, tiles["tb"])
    merged = _merge(g5, yb, yc, sg, p["s5_wglu"], p["s5_bglu"], p["wb_a"], p["wb_b"], p["wb_c"], tiles["tm_merge"])
    x = _proj_res(merged, p["w_out"], x, tiles["tm_merge"])
    return _ffn(x, p["ffn_norm"], p["ffn_wup"], p["ffn_cv"], p["ffn_wd"], p["final_norm"], seq, tiles["tm_ffn"], last)


def _tiles(seq):
    return {
        "tm_proj": min(1024, seq), "tn_mixer": MIXER_COLS // 4, "tn_gates": GATE_COLS // 3,
        "tb": min(512, seq), "tm_merge": min(512, seq), "tm_ffn": min(1024, seq),
    }


def kernel(x, mix_norm, w_in, s5_lam_re, s5_lam_im, s5_log_dt, s5_b_re, s5_b_im, s5_c_re, s5_c_im, s5_d, s5_w_glu, s5_b_glu, hg_lower_bounds, hg_norm, ml_conv_w, ml_conv_b, ml_w_qk, ml_b_ig, ml_b_fg, ml_norm, w_branch, w_out, ffn_norm, ffn_w_up, ffn_conv_w, ffn_conv_b, ffn_w_down, final_norm):
    nb, seq, d = x.shape
    depth = w_in.shape[0]
    tiles = _tiles(seq)
    lbs = jax.nn.softmax(hg_lower_bounds.astype(f32), axis=0)
    lbs = jnp.cumsum(lbs, axis=0) - lbs[0:1]
    xs = x.astype(f32).reshape(nb * seq, d)
    for l in range(depth):
        w_mixer, w_gates = _pack_w_in(w_in[l])
        s5_m, s5_bst, s5_cst, s5_av = _s5_prep(
            s5_lam_re[l], s5_lam_im[l], s5_log_dt[l], s5_b_re[l], s5_b_im[l], s5_c_re[l], s5_c_im[l], s5_d[l])
        wqk = ml_w_qk[l]
        wq = jnp.pad(wqk[:, :, :ML_DH], ((0, 0), (0, ML_DP - ML_DH), (0, ML_DP - ML_DH)))
        wk = jnp.pad(wqk[:, :, ML_DH:], ((0, 0), (0, ML_DP - ML_DH), (0, ML_DP - ML_DH)))
        gbias = jnp.concatenate([ml_b_ig[l], ml_b_fg[l], jnp.zeros((128 - 2 * ML_HEADS,), f32)])[None, :]
        wb = w_branch[l]
        ffn_cv = jnp.concatenate([ffn_conv_w[l], ffn_conv_b[l][None, :],
                                  jnp.zeros((8 - FFN_CONV - 1, 2 * FFN_DIM), f32)], axis=0)
        p = {
            "mix_norm": mix_norm[l][None, :], "w_mixer": w_mixer, "w_gates": w_gates,
            "s5_m": s5_m, "s5_bst": s5_bst, "s5_cst": s5_cst, "s5_av": s5_av,
            "s5_wglu": s5_w_glu[l].astype(bf16), "s5_bglu": s5_b_glu[l][None, :],
            "hg_lb": lbs[l][None, :], "hg_norm": hg_norm[l][None, :],
            "ml_gb": gbias, "ml_cw": _pad_heads(ml_conv_w[l], 1), "ml_cb": _pad_heads(ml_conv_b[l][None, :], 1),
            "ml_wqk": jnp.concatenate([wq, wk], axis=-1).astype(bf16),
            "ml_norm": _pad_heads(ml_norm[l][None, :], 1),
            "wb_a": wb[:S5_WIDTH].astype(bf16), "wb_b": wb[S5_WIDTH:S5_WIDTH + HG_WIDTH].astype(bf16),
            "wb_c": _pad_heads(wb[S5_WIDTH + HG_WIDTH:], 0).astype(bf16),
            "w_out": w_out[l].astype(bf16),
            "ffn_norm": ffn_norm[l][None, :], "ffn_wup": ffn_w_up[l].astype(bf16), "ffn_cv": ffn_cv, "ffn_wd": ffn_w_down[l].astype(bf16),
            "final_norm": final_norm[None, :],
        }
        xs = _layer(xs, p, nb, seq, tiles, l == depth - 1)
    return xs.reshape(nb, seq, d).astype(x.dtype)
```

```python
import functools

import jax
import jax.numpy as jnp
from jax import lax
from jax.experimental import pallas as pl
from jax.experimental.pallas import tpu as pltpu

f32 = jnp.float32
bf16 = jnp.bfloat16

D_MODEL = 2048
EPS = 1e-6
CHUNK = 64

S5_GROUPS = 32
S5_P = 16
S5_WIDTH = 512
S5_STATE = 64
S5_L = 8
S5_OCT = 8

HG_HEADS = 6
HG_D = 128
HG_WIDTH = 768
HG_SUB = 8
HG_PER = 3

ML_HEADS = 4
ML_DH = 192
ML_DP = 256
ML_WIDTH = 768
ML_WP = ML_HEADS * ML_DP
ML_CONV = 4
ML_PER = 2

FFN_DIM = 5632
FFN_CONV = 3
FFN_TN = 512
FFN_NJ = FFN_DIM // FFN_TN

OFF_U, OFF_Q, OFF_F, OFF_I, OFF_OG = 0, 512, 1280, 2048, 2816
OFF_CX, OFF_V, OFF_O, OFF_IGFG = 3584, 4608, 5632, 6656
MIXER_COLS = 7168
GATE_COLS = 3 * D_MODEL

VMEM_LIMIT = 60 * 1024 * 1024


def _split3(x):
    hi = x.astype(bf16)
    r1 = x - hi.astype(f32)
    mid = r1.astype(bf16)
    lo = (r1 - mid.astype(f32)).astype(bf16)
    return hi, mid, lo


def _dot(a, b):
    return jnp.dot(a, b, preferred_element_type=f32)


def _dot_nt(a, b):
    return lax.dot_general(a, b, (((1,), (1,)), ((), ())), preferred_element_type=f32)


def _dot_tn(a, b):
    return lax.dot_general(a, b, (((0,), (0,)), ((), ())), preferred_element_type=f32)


def _exact_dot(a_bf16, x):
    hi, mid, lo = _split3(x)
    return _dot(a_bf16, hi) + _dot(a_bf16, mid) + _dot(a_bf16, lo)


def _rms(x, g):
    ms = jnp.mean(x * x, axis=-1, keepdims=True)
    return x * lax.rsqrt(ms + EPS) * g


def _norm_proj_kernel(x_ref, g_ref, w_ref, o_ref, hs_ref, *, act):
    @pl.when(pl.program_id(1) == 0)
    def _():
        hs_ref[...] = _rms(x_ref[...], g_ref[...]).astype(bf16)

    y = _dot(hs_ref[...], w_ref[...])
    if act == "sigmoid":
        y = jax.nn.sigmoid(y)
    o_ref[...] = y.astype(o_ref.dtype)


def _norm_proj(x, g, w, layer, *, act, out_dtype, tm, tn):
    t, d = x.shape
    n = w.shape[2]
    return pl.pallas_call(
        functools.partial(_norm_proj_kernel, act=act),
        grid=(t // tm, n // tn),
        in_specs=[
            pl.BlockSpec((tm, d), lambda i, j: (i, 0)),
            pl.BlockSpec((1, d), lambda i, j: (0, 0)),
            pl.BlockSpec((None, d, tn), lambda i, j: (layer, 0, j)),
        ],
        out_specs=pl.BlockSpec((tm, tn), lambda i, j: (i, j)),
        out_shape=jax.ShapeDtypeStruct((t, n), out_dtype),
        scratch_shapes=[pltpu.VMEM((tm, d), bf16)],
        compiler_params=pltpu.CompilerParams(dimension_semantics=("parallel", "arbitrary"),
                                             vmem_limit_bytes=VMEM_LIMIT),
        name="norm_proj_" + act,
    )(x, g, w)


def _s5_kernel(u_ref, m_ref, bst_ref, cst_ref, av_ref, y_ref, sin_ref, xp_ref):
    nc = u_ref.shape[0] // S5_L
    half = S5_OCT * S5_STATE
    xb = jnp.concatenate([u_ref[pl.ds(t, nc, stride=S5_L), :] for t in range(S5_L)], axis=1).astype(bf16)
    sin_ref[...] = _dot(xb, bst_ref[0])
    ar = av_ref[0, 0:1, :]
    ai = av_ref[0, 1:2, :]

    def step(c, carry):
        xr, xi = carry
        xp_ref[pl.ds(c, 1), :] = jnp.concatenate([xr, xi], axis=1)
        s = sin_ref[pl.ds(c, 1), :]
        return ar * xr - ai * xi + s[:, :half], ar * xi + ai * xr + s[:, half:]

    z = jnp.zeros((1, half), f32)
    lax.fori_loop(0, nc, step, (z, z), unroll=8)
    y = jax.nn.gelu(_dot(xb, m_ref[0]) + _dot(xp_ref[...].astype(bf16), cst_ref[0]))
    for t in range(S5_L):
        y_ref[pl.ds(t, nc, stride=S5_L), :] = y[:, t * 128:(t + 1) * 128]


def _s5_core(proj, m, bst, cst, av, nb):
    t = proj.shape[0]
    seq = t // nb
    nc = seq // S5_L
    noct = S5_GROUPS // S5_OCT
    cols = S5_L * 128
    st = 2 * S5_OCT * S5_STATE
    return pl.pallas_call(
        _s5_kernel,
        grid=(noct, nb),
        in_specs=[
            pl.BlockSpec((seq, 128), lambda o, b: (b, OFF_U // 128 + o)),
            pl.BlockSpec((1, cols, cols), lambda o, b: (o, 0, 0)),
            pl.BlockSpec((1, cols, st), lambda o, b: (o, 0, 0)),
            pl.BlockSpec((1, st, cols), lambda o, b: (o, 0, 0)),
            pl.BlockSpec((1, 8, st // 2), lambda o, b: (o, 0, 0)),
        ],
        out_specs=pl.BlockSpec((seq, 128), lambda o, b: (b, o)),
        out_shape=jax.ShapeDtypeStruct((t, S5_WIDTH), f32),
        scratch_shapes=[pltpu.VMEM((nc, st), f32), pltpu.VMEM((nc, st), f32)],
        compiler_params=pltpu.CompilerParams(dimension_semantics=("parallel", "parallel")),
        name="s5_core",
    )(proj, m, bst, cst, av)


def _s5_prep(lam_re, lam_im, log_dt, b_re, b_im, c_re, c_im, d_skip):
    hp = lax.Precision.HIGHEST
    g, n, p, ln, oc = S5_GROUPS, S5_STATE, S5_P, S5_L, S5_OCT
    no = g // oc
    dt = jnp.exp(log_dt)[:, None]
    lr, li = lam_re, lam_im
    mag = jnp.exp(lr * dt)
    ar, ai = mag * jnp.cos(li * dt), mag * jnp.sin(li * dt)
    den = lr * lr + li * li
    cr = ((ar - 1.0) * lr + ai * li) / den
    ci = (ai * lr - (ar - 1.0) * li) / den
    bbr = cr[..., None] * b_re - ci[..., None] * b_im
    bbi = cr[..., None] * b_im + ci[..., None] * b_re
    tau = jnp.arange(ln + 1, dtype=f32)[:, None, None]
    pmag = jnp.exp(tau * (lr * dt)[None])
    pr, pi = pmag * jnp.cos(tau * (li * dt)[None]), pmag * jnp.sin(tau * (li * dt)[None])
    cpr = c_re[None] * pr[:, :, None, :] - c_im[None] * pi[:, :, None, :]
    cpi = c_re[None] * pi[:, :, None, :] + c_im[None] * pr[:, :, None, :]
    kern = (jnp.einsum("tgpn,gnq->tgpq", cpr[:ln], bbr, precision=hp)
            - jnp.einsum("tgpn,gnq->tgpq", cpi[:ln], bbi, precision=hp))
    dskip = d_skip.reshape(g, p)
    kern = kern.at[0].add(jnp.eye(p, dtype=f32)[None] * dskip[:, :, None])
    kern = jnp.concatenate([kern, jnp.zeros((1, g, p, p), f32)], axis=0)
    eye = jnp.eye(oc, dtype=bf16)
    blk = kern.reshape(ln + 1, no, oc, p, p).transpose(0, 1, 2, 4, 3)
    blk = (blk.astype(bf16)[:, :, :, :, None, :] * eye[None, None, :, None, :, None]).reshape(ln + 1, no, oc * p, oc * p)
    s_idx = jnp.arange(ln)[:, None]
    t_idx = jnp.arange(ln)[None, :]
    lag = jnp.where(t_idx >= s_idx, t_idx - s_idx, ln)
    m8 = blk[lag].transpose(2, 0, 3, 1, 4).reshape(no, ln * oc * p, ln * oc * p)
    tau_b = (ln - 1) - tau[:ln]
    bmag = jnp.exp(tau_b * (lr * dt)[None])
    prs, pis = bmag * jnp.cos(tau_b * (li * dt)[None]), bmag * jnp.sin(tau_b * (li * dt)[None])
    bbrt, bbit = bbr.transpose(0, 2, 1), bbi.transpose(0, 2, 1)
    bst_r = prs[:, :, None, :] * bbrt[None] - pis[:, :, None, :] * bbit[None]
    bst_i = prs[:, :, None, :] * bbit[None] + pis[:, :, None, :] * bbrt[None]

    def expand_b(x):
        x = x.astype(bf16).reshape(ln, no, oc, p, n)
        return (x[:, :, :, :, None, :] * eye[None, None, :, None, :, None]).reshape(ln, no, oc * p, oc * n)

    bst8 = jnp.concatenate([expand_b(bst_r), expand_b(bst_i)], axis=-1)
    bst8 = bst8.transpose(1, 0, 2, 3).reshape(no, ln * oc * p, 2 * oc * n)
    cst = jnp.stack([cpr[1:], -cpi[1:]], axis=0)
    cst = cst.astype(bf16).reshape(2, ln, no, oc, p, n).transpose(0, 1, 2, 3, 5, 4)
    cst8 = cst[:, :, :, :, :, None, :] * eye[None, None, None, :, None, :, None]
    cst8 = cst8.reshape(2, ln, no, oc * n, oc * p).transpose(2, 0, 3, 1, 4).reshape(no, 2 * oc * n, ln * oc * p)
    av = jnp.stack([pr[ln].reshape(no, oc * n), pi[ln].reshape(no, oc * n)], axis=1)
    av = jnp.concatenate([av, jnp.zeros((no, 6, oc * n), f32)], axis=1)
    return m8, bst8, cst8, av


def _hg_consts():
    row = lax.broadcasted_iota(jnp.int32, (CHUNK, CHUNK), 0)
    col = lax.broadcasted_iota(jnp.int32, (CHUNK, CHUNK), 1)
    diag_mask = ((row // HG_SUB) == (col // HG_SUB)) & (col <= row)
    levels = []
    half = HG_SUB
    while half < CHUNK:
        levels.append((half, ((row // (2 * half)) == (col // (2 * half)))
                       & ((row & half) != 0) & ((col & half) == 0)))
        half *= 2
    wrow = lax.broadcasted_iota(jnp.int32, (HG_SUB * HG_D, CHUNK), 0)
    wcol = lax.broadcasted_iota(jnp.int32, (HG_SUB * HG_D, CHUNK), 1)
    wsel = ((wcol % HG_SUB) == (wrow // HG_D)).astype(bf16)
    return diag_mask, levels, wsel


def _hg_prelude(f_ref, lb, tril, k_s, cum_s):
    tb = f_ref.shape[0]
    f = lb + (1.0 - lb) * jax.nn.sigmoid(f_ref[...])
    k_s[...] = 1.0 - f
    logf = jnp.log2(f)
    for c in range(tb // CHUNK):
        cum_s[c * CHUNK:(c + 1) * CHUNK, :] = _exact_dot(tril, logf[c * CHUNK:(c + 1) * CHUNK])


def _hg_chunk(c, q_ref, i_ref, og_ref, ng, y_ref, lanes, k_s, cum_s, st, consts):
    diag_mask, levels, wsel = consts
    nblk = CHUNK // HG_SUB
    r0 = c * CHUNK
    q = jax.nn.silu(q_ref[r0:r0 + CHUNK, :])
    k = k_s[r0:r0 + CHUNK, :]
    cum = cum_s[r0:r0 + CHUNK, :]
    vb = i_ref[r0:r0 + CHUNK, :].astype(bf16)
    last = cum_s[r0 + CHUNK - 1:r0 + CHUNK, :]
    inter = _dot_nt((q * jnp.exp2(cum)).astype(bf16), st.astype(bf16))
    slabs = []
    for s in range(HG_SUB):
        pieces = []
        for b in range(nblk):
            rr = r0 + b * HG_SUB + s
            blk = slice(b * HG_SUB, (b + 1) * HG_SUB)
            dec = jnp.exp2(jnp.minimum(cum[blk] - cum_s[rr:rr + 1, :], 0.0))
            pieces.append((q[blk] * k_s[rr:rr + 1, :]) * dec)
        slabs.append(jnp.concatenate(pieces, axis=0).astype(bf16))
    sc = jnp.where(diag_mask, _dot(jnp.concatenate(slabs, axis=1), wsel), 0.0)
    for half, mask in levels:
        zs = []
        for base in range(0, CHUNK, 2 * half):
            mid = cum_s[r0 + base + half - 1:r0 + base + half, :]
            lo, hi = slice(base, base + half), slice(base + half, base + 2 * half)
            zs.append(k[lo] * jnp.exp2(mid - cum[lo]))
            zs.append(q[hi] * jnp.exp2(cum[hi] - mid))
        z = jnp.concatenate(zs, axis=0).astype(bf16)
        sc = jnp.where(mask, _dot_nt(z, z), sc)
    out = inter + _dot(sc.astype(bf16), vb)
    kdec = (k * jnp.exp2(last - cum)).astype(bf16)
    st = st * jnp.exp2(last) + _dot_tn(vb, kdec)
    o = _rms(out, ng) * jax.nn.silu(og_ref[r0:r0 + CHUNK, :])
    y_ref[r0:r0 + CHUNK, lanes] = o.astype(y_ref.dtype)
    return st


def _ml_prelude(cx_ref, lanes, gt_ref, gb, head, cw, cb, wqk, tril, carry_ref, buf_ref, q_s, k_s):
    tb = gt_ref.shape[0]
    nch = tb // CHUNK
    lane = lax.broadcasted_iota(jnp.int32, (CHUNK, 128), 1)
    lane_b = lax.broadcasted_iota(jnp.int32, (tb, 128), 1)
    gts = gt_ref[...] + gb
    ib_all = jnp.sum(jnp.where(lane_b == head, gts, 0.0), axis=-1, keepdims=True)
    lf_all = jax.nn.log_sigmoid(jnp.sum(jnp.where(lane_b == head + ML_HEADS, gts, 0.0), axis=-1, keepdims=True))
    lfc = jnp.zeros((CHUNK, 128), f32)
    ibc = jnp.zeros((CHUNK, 128), f32)
    for c in range(nch):
        lfc = jnp.where(lane == c, lf_all[c * CHUNK:(c + 1) * CHUNK], lfc)
        ibc = jnp.where(lane == c, ib_all[c * CHUNK:(c + 1) * CHUNK], ibc)
    bc = _exact_dot(tril, lfc)
    zt = (ibc - bc).T
    cx = cx_ref[:, lanes]
    buf_ref[0:8, :] = carry_ref[...]
    buf_ref[8:, :] = cx
    carry_ref[...] = cx[tb - 8:, :]
    ca = cw[ML_CONV - 1:ML_CONV, :] * cx + cb
    for kk in range(ML_CONV - 1):
        sh = ML_CONV - 1 - kk
        ca = ca + cw[kk:kk + 1, :] * buf_ref[pl.ds(8 - sh, tb), :]
    qk = _dot(jax.nn.silu(ca).astype(bf16), wqk)
    q_s[...] = qk[:, :ML_DP].astype(bf16)
    k_s[...] = (qk[:, ML_DP:] * (ML_DH ** -0.5)).astype(bf16)
    return ib_all, bc, zt


def _ml_chunk(c, gates, v_ref, o_ref, lanes, ng, y_ref, q_s, k_s, cmat, m_prev):
    ib_all, bc, zt = gates
    row = lax.broadcasted_iota(jnp.int32, (CHUNK, CHUNK), 0)
    col = lax.broadcasted_iota(jnp.int32, (CHUNK, CHUNK), 1)
    lane_w = lax.broadcasted_iota(jnp.int32, (CHUNK, ML_DP), 1)
    r0 = c * CHUNK
    ib = ib_all[r0:r0 + CHUNK]
    bcum = bc[:, c:c + 1]
    dmat = jnp.where(row >= col, bcum + zt[c:c + 1, :], -jnp.inf)
    m_t = jnp.maximum(bcum + m_prev, jnp.max(dmat, axis=-1, keepdims=True))
    m_new = m_t[CHUNK - 1:CHUNK, :]
    b_last = bcum[CHUNK - 1:CHUNK, :]
    q = q_s[r0:r0 + CHUNK, :]
    k = k_s[r0:r0 + CHUNK, :]
    vaug = jnp.where(lane_w == ML_DH, 1.0, v_ref[r0:r0 + CHUNK, lanes]).astype(bf16)
    w_inter = jnp.exp(bcum + m_prev - m_t)
    w_intra = jnp.exp(dmat - m_t) * _dot_nt(q, k)
    num = w_inter * _dot(q, cmat.astype(bf16)) + _dot(w_intra.astype(bf16), vaug)
    decay = jnp.exp(b_last + m_prev - m_new)
    ws = jnp.exp(b_last - bcum + ib - m_new)
    cmat = decay * cmat + _dot_tn((ws * k.astype(f32)).astype(bf16), vaug)
    den = jnp.sum(jnp.where(lane_w == ML_DH, num, 0.0), axis=-1, keepdims=True)
    h = num / jnp.maximum(jnp.abs(den), jnp.exp(-m_t))
    hm = jnp.where(lane_w < ML_DH, h, 0.0)
    ms = jnp.sum(hm * hm, axis=-1, keepdims=True) * (1.0 / ML_DH)
    hn = hm * lax.rsqrt(ms + EPS) * ng
    y_ref[r0:r0 + CHUNK, lanes] = (hn * jax.nn.sigmoid(o_ref[r0:r0 + CHUNK, lanes])).astype(y_ref.dtype)
    return cmat, m_new


def _mixers_kernel(*refs):
    hg_in = refs[:4 * HG_PER]
    lb_ref, hng_ref, cx_ref, v_ref, o_ref, gt_ref, gb_ref, cw_ref, cb_ref, wqk_ref, mng_ref = refs[4 * HG_PER:4 * HG_PER + 11]
    yb_ref, yc_ref = refs[4 * HG_PER + 11:4 * HG_PER + 13]
    st_ref, hk_s, cum_s, c_ref, m_ref, carry_ref, buf_ref, q_s, k_s = refs[4 * HG_PER + 13:]
    tb = gt_ref.shape[0]
    group = pl.program_id(1)

    @pl.when(pl.program_id(2) == 0)
    def _():
        st_ref[...] = jnp.zeros_like(st_ref)
        c_ref[...] = jnp.zeros_like(c_ref)
        m_ref[...] = jnp.zeros_like(m_ref)
        carry_ref[...] = jnp.zeros_like(carry_ref)

    row = lax.broadcasted_iota(jnp.int32, (CHUNK, CHUNK), 0)
    col = lax.broadcasted_iota(jnp.int32, (CHUNK, CHUNK), 1)
    tril = (row >= col).astype(bf16)
    consts = _hg_consts()
    gb = gb_ref[...]
    hg_lanes = [slice(h * HG_D, (h + 1) * HG_D) for h in range(HG_PER)]
    ml_lanes = [slice(h * ML_DP, (h + 1) * ML_DP) for h in range(ML_PER)]

    for h in range(HG_PER):
        _hg_prelude(hg_in[4 * h + 1], lb_ref[:, hg_lanes[h]], tril, hk_s.at[h], cum_s.at[h])
    gates = []
    for h in range(ML_PER):
        gates.append(_ml_prelude(cx_ref, ml_lanes[h], gt_ref, gb, group * ML_PER + h, cw_ref[:, ml_lanes[h]],
                                 cb_ref[:, ml_lanes[h]], wqk_ref[h], tril, carry_ref.at[h], buf_ref.at[h],
                                 q_s.at[h], k_s.at[h]))

    sts = [st_ref[h] for h in range(HG_PER)]
    cms = [c_ref[h] for h in range(ML_PER)]
    mps = [m_ref[h][0:1, 0:1] for h in range(ML_PER)]
    for c in range(tb // CHUNK):
        for h in range(max(HG_PER, ML_PER)):
            if h < HG_PER:
                sts[h] = _hg_chunk(c, hg_in[4 * h], hg_in[4 * h + 2], hg_in[4 * h + 3], hng_ref[:, hg_lanes[h]], yb_ref,
                                   hg_lanes[h], hk_s.at[h], cum_s.at[h], sts[h], consts)
            if h < ML_PER:
                cms[h], mps[h] = _ml_chunk(c, gates[h], v_ref, o_ref, ml_lanes[h], mng_ref[:, ml_lanes[h]], yc_ref,
                                           q_s.at[h], k_s.at[h], cms[h], mps[h])
    for h in range(HG_PER):
        st_ref[h] = sts[h]
    for h in range(ML_PER):
        c_ref[h] = cms[h]
        m_ref[h] = jnp.broadcast_to(mps[h], m_ref.shape[1:])


def _mixers(proj, lb, hng, gb, cw, cb, wqk, mng, nb, seq, tb):
    t = proj.shape[0]
    nj = seq // tb
    hg_w, ml_w = HG_PER * HG_D, ML_PER * ML_DP

    def hseg(off, h):
        return pl.BlockSpec((tb, HG_D), lambda b, g, j, o=off // HG_D + h: (b * nj + j, o + HG_PER * g))

    def mseg(off):
        return pl.BlockSpec((tb, ml_w), lambda b, g, j, o=off // ml_w: (b * nj + j, o + g))

    in_specs = []
    for h in range(HG_PER):
        in_specs += [hseg(OFF_Q, h), hseg(OFF_F, h), hseg(OFF_I, h), hseg(OFF_OG, h)]
    in_specs += [
        pl.BlockSpec((1, hg_w), lambda b, g, j: (0, g)),
        pl.BlockSpec((1, hg_w), lambda b, g, j: (0, g)),
        mseg(OFF_CX), mseg(OFF_V), mseg(OFF_O),
        pl.BlockSpec((tb, 128), lambda b, g, j: (b * nj + j, OFF_IGFG // 128)),
        pl.BlockSpec((1, 128), lambda b, g, j: (0, 0)),
        pl.BlockSpec((ML_CONV, ml_w), lambda b, g, j: (0, g)),
        pl.BlockSpec((1, ml_w), lambda b, g, j: (0, g)),
        pl.BlockSpec((ML_PER, ML_DP, 2 * ML_DP), lambda b, g, j: (g, 0, 0)),
        pl.BlockSpec((1, ml_w), lambda b, g, j: (0, g)),
    ]
    return pl.pallas_call(
        _mixers_kernel,
        grid=(nb, HG_HEADS // HG_PER, nj),
        in_specs=in_specs,
        out_specs=[pl.BlockSpec((tb, hg_w), lambda b, g, j: (b * nj + j, g)),
                   pl.BlockSpec((tb, ml_w), lambda b, g, j: (b * nj + j, g))],
        out_shape=[jax.ShapeDtypeStruct((t, HG_WIDTH), bf16), jax.ShapeDtypeStruct((t, ML_WP), bf16)],
        scratch_shapes=[
            pltpu.VMEM((HG_PER, HG_D, HG_D), f32),
            pltpu.VMEM((HG_PER, tb, HG_D), f32),
            pltpu.VMEM((HG_PER, tb, HG_D), f32),
            pltpu.VMEM((ML_PER, ML_DP, ML_DP), f32),
            pltpu.VMEM((ML_PER, 8, 128), f32),
            pltpu.VMEM((ML_PER, 8, ML_DP), f32),
            pltpu.VMEM((ML_PER, tb + 8, ML_DP), f32),
            pltpu.VMEM((ML_PER, tb, ML_DP), bf16),
            pltpu.VMEM((ML_PER, tb, ML_DP), bf16),
        ],
        compiler_params=pltpu.CompilerParams(dimension_semantics=("parallel", "parallel", "arbitrary")),
        name="mixers",
    )(*([proj] * (4 * HG_PER)), lb, hng, proj, proj, proj, proj, gb, cw, cb, wqk, mng)


def _merge_kernel(g5_ref, yb_ref, yc_ref, sa_ref, sb_ref, sc_ref, wglu_ref, bglu_ref,
                  wa_ref, wb_ref, wc_ref, o_ref):
    g5 = g5_ref[...]
    ya = g5 * jax.nn.sigmoid(_dot(g5.astype(bf16), wglu_ref[...]) + bglu_ref[...])
    m = sa_ref[...].astype(f32) * _dot(ya.astype(bf16), wa_ref[...])
    m = m + sb_ref[...].astype(f32) * _dot(yb_ref[...], wb_ref[...])
    m = m + sc_ref[...].astype(f32) * _dot(yc_ref[...], wc_ref[...])
    o_ref[...] = m.astype(o_ref.dtype)


def _merge(g5, yb, yc, sg, wglu, bglu, wa, wb, wc, tm):
    t = g5.shape[0]
    d = D_MODEL

    def rows(w):
        return pl.BlockSpec((tm, w), lambda i: (i, 0))

    def full(a):
        return pl.BlockSpec(a.shape, lambda i: (0, 0))

    return pl.pallas_call(
        _merge_kernel,
        grid=(t // tm,),
        in_specs=[
            rows(S5_WIDTH), rows(HG_WIDTH), rows(ML_WP),
            pl.BlockSpec((tm, d), lambda i: (i, 0)),
            pl.BlockSpec((tm, d), lambda i: (i, 1)),
            pl.BlockSpec((tm, d), lambda i: (i, 2)),
            full(wglu), full(bglu), full(wa), full(wb), full(wc),
        ],
        out_specs=rows(d),
        out_shape=jax.ShapeDtypeStruct((t, d), bf16),
        compiler_params=pltpu.CompilerParams(dimension_semantics=("parallel",)),
        name="merge",
    )(g5, yb, yc, sg, sg, sg, wglu, bglu, wa, wb, wc)


def _proj_res_kernel(a_ref, w_ref, r_ref, o_ref):
    o_ref[...] = r_ref[...] + _dot(a_ref[...], w_ref[...])


def _proj_res(a, w, layer, res, tm):
    t, k = a.shape
    n = w.shape[2]
    return pl.pallas_call(
        _proj_res_kernel,
        grid=(t // tm,),
        in_specs=[
            pl.BlockSpec((tm, k), lambda i: (i, 0)),
            pl.BlockSpec((None, k, n), lambda i: (layer, 0, 0)),
            pl.BlockSpec((tm, n), lambda i: (i, 0)),
        ],
        out_specs=pl.BlockSpec((tm, n), lambda i: (i, 0)),
        out_shape=jax.ShapeDtypeStruct((t, n), f32),
        compiler_params=pltpu.CompilerParams(dimension_semantics=("parallel",)),
        name="proj_res",
    )(a, w, res)


def _ffn_kernel(x_ref, g_ref, wa_ref, wb_ref, cva_ref, cvb_ref, wd_ref, fg_ref, o_ref,
                hs_ref, bufa_ref, bufb_ref, cara_ref, carb_ref, *, tiles_per_seq, final):
    tm = x_ref.shape[0]
    i = pl.program_id(0)
    j = pl.program_id(1)

    @pl.when(j == 0)
    def _():
        x = x_ref[...]
        hs_ref[...] = _rms(x, g_ref[...]).astype(bf16)
        o_ref[...] = x

    @pl.when(i % tiles_per_seq == 0)
    def _():
        cara_ref[j] = jnp.zeros((8, FFN_TN), f32)
        carb_ref[j] = jnp.zeros((8, FFN_TN), f32)

    hs = hs_ref[...]

    def conv(w_ref, cv_ref, buf_ref, car_ref):
        u = _dot(hs, w_ref[...])
        buf_ref[0:8, :] = car_ref[j]
        buf_ref[8:, :] = u
        car_ref[j] = u[tm - 8:, :]
        y = cv_ref[FFN_CONV - 1:FFN_CONV, :] * u + cv_ref[FFN_CONV:FFN_CONV + 1, :]
        for kk in range(FFN_CONV - 1):
            sh = FFN_CONV - 1 - kk
            y = y + cv_ref[kk:kk + 1, :] * buf_ref[pl.ds(8 - sh, tm), :]
        return y

    a = conv(wa_ref, cva_ref, bufa_ref, cara_ref)
    b = conv(wb_ref, cvb_ref, bufb_ref, carb_ref)
    act = (jax.nn.silu(a) * b).astype(bf16)
    o_ref[...] += _dot(act, wd_ref[...])

    if final:
        @pl.when(j == FFN_NJ - 1)
        def _():
            o_ref[...] = _rms(o_ref[...], fg_ref[...])


def _ffn(x, g, wup, cv, wd, layer, fg, seq, tm, final):
    t, d = x.shape
    return pl.pallas_call(
        functools.partial(_ffn_kernel, tiles_per_seq=seq // tm, final=final),
        grid=(t // tm, FFN_NJ),
        in_specs=[
            pl.BlockSpec((tm, d), lambda i, j: (i, 0), pipeline_mode=pl.Buffered(1)),
            pl.BlockSpec((1, d), lambda i, j: (0, 0)),
            pl.BlockSpec((None, d, FFN_TN), lambda i, j: (layer, 0, j)),
            pl.BlockSpec((None, d, FFN_TN), lambda i, j: (layer, 0, j + FFN_NJ)),
            pl.BlockSpec((8, FFN_TN), lambda i, j: (0, j)),
            pl.BlockSpec((8, FFN_TN), lambda i, j: (0, j + FFN_NJ)),
            pl.BlockSpec((None, FFN_TN, d), lambda i, j: (layer, j, 0)),
            pl.BlockSpec((1, d), lambda i, j: (0, 0)),
        ],
        out_specs=pl.BlockSpec((tm, d), lambda i, j: (i, 0)),
        out_shape=jax.ShapeDtypeStruct((t, d), f32),
        scratch_shapes=[
            pltpu.VMEM((tm, d), bf16),
            pltpu.VMEM((tm + 8, FFN_TN), f32),
            pltpu.VMEM((tm + 8, FFN_TN), f32),
            pltpu.VMEM((FFN_NJ, 8, FFN_TN), f32),
            pltpu.VMEM((FFN_NJ, 8, FFN_TN), f32),
        ],
        compiler_params=pltpu.CompilerParams(dimension_semantics=("arbitrary", "arbitrary"),
                                             vmem_limit_bytes=VMEM_LIMIT),
        name="ffn",
    )(x, g, wup, wup, cv, cv, wd, fg)


def _pad_heads(w, axis):
    shp = w.shape
    w = w.reshape(shp[:axis] + (ML_HEADS, ML_DH) + shp[axis + 1:])
    pad = [(0, 0)] * w.ndim
    pad[axis + 1] = (0, ML_DP - ML_DH)
    w = jnp.pad(w, pad)
    return w.reshape(shp[:axis] + (ML_WP,) + shp[axis + 1:])


def _pack_w_in(w):
    w = w.astype(bf16)
    parts = [
        w[..., 0:3584],
        _pad_heads(w[..., 3584:4352], 2),
        _pad_heads(w[..., 4352:5120], 2),
        _pad_heads(w[..., 5120:5888], 2),
        w[..., 5888:5896],
        jnp.zeros(w.shape[:2] + (MIXER_COLS - OFF_IGFG - 2 * ML_HEADS,), w.dtype),
    ]
    return jnp.concatenate(parts, axis=-1), w[..., 5896:]


def _layer(x, p, big, layer, nb, seq, tiles, last):
    proj = _norm_proj(x, p["mix_norm"], big["w_mixer"], layer, act="none", out_dtype=f32,
                      tm=tiles["tm_proj"], tn=tiles["tn_mixer"])
    sg = _norm_proj(x, p["mix_norm"], big["w_gates"], layer, act="sigmoid", out_dtype=bf16,
                    tm=tiles["tm_proj"], tn=tiles["tn_gates"])
    g5 = _s5_core(proj, p["s5_m"], p["s5_bst"], p["s5_cst"], p["s5_av"], nb)
    yb, yc = _mixers(proj, p["hg_lb"], p["hg_norm"], p["ml_gb"], p["ml_cw"], p["ml_cb"], p["ml_wqk"], p["ml_norm"],
                     nb, seq, tiles["tb"])
    merged = _merge(g5, yb, yc, sg, p["s5_wglu"], p["s5_bglu"], p["wb_a"], p["wb_b"], p["wb_c"], tiles["tm_merge"])
    x = _proj_res(merged, big["w_out"], layer, x, tiles["tm_merge"])
    return _ffn(x, p["ffn_norm"], big["ffn_wup"], p["ffn_cv"], big["ffn_wd"], layer, p["final_norm"], seq,
                tiles["tm_ffn"], last)


def _tiles(seq):
    return {
        "tm_proj": min(1024, seq), "tn_mixer": MIXER_COLS // 4, "tn_gates": GATE_COLS // 3,
        "tb": min(512, seq), "tm_merge": min(512, seq), "tm_ffn": min(1024, seq),
    }


def kernel(x, mix_norm, w_in, s5_lam_re, s5_lam_im, s5_log_dt, s5_b_re, s5_b_im, s5_c_re, s5_c_im, s5_d, s5_w_glu, s5_b_glu, hg_lower_bounds, hg_norm, ml_conv_w, ml_conv_b, ml_w_qk, ml_b_ig, ml_b_fg, ml_norm, w_branch, w_out, ffn_norm, ffn_w_up, ffn_conv_w, ffn_conv_b, ffn_w_down, final_norm):
    nb, seq, d = x.shape
    depth = w_in.shape[0]
    tiles = _tiles(seq)
    lbs = jax.nn.softmax(hg_lower_bounds.astype(f32), axis=0)
    lbs = jnp.cumsum(lbs, axis=0) - lbs[0:1]
    xs = x.astype(f32).reshape(nb * seq, d)
    w_mixer, w_gates = _pack_w_in(w_in)
    big = {"w_mixer": w_mixer, "w_gates": w_gates, "w_out": w_out.astype(bf16),
           "ffn_wup": ffn_w_up.astype(bf16), "ffn_wd": ffn_w_down.astype(bf16)}
    for l in range(depth):
        s5_m, s5_bst, s5_cst, s5_av = _s5_prep(
            s5_lam_re[l], s5_lam_im[l], s5_log_dt[l], s5_b_re[l], s5_b_im[l], s5_c_re[l], s5_c_im[l], s5_d[l])
        wqk = ml_w_qk[l]
        wq = jnp.pad(wqk[:, :, :ML_DH], ((0, 0), (0, ML_DP - ML_DH), (0, ML_DP - ML_DH)))
        wk = jnp.pad(wqk[:, :, ML_DH:], ((0, 0), (0, ML_DP - ML_DH), (0, ML_DP - ML_DH)))
        gbias = jnp.concatenate([ml_b_ig[l], ml_b_fg[l], jnp.zeros((128 - 2 * ML_HEADS,), f32)])[None, :]
        wb = w_branch[l]
        ffn_cv = jnp.concatenate([ffn_conv_w[l], ffn_conv_b[l][None, :],
                                  jnp.zeros((8 - FFN_CONV - 1, 2 * FFN_DIM), f32)], axis=0)
        p = {
            "mix_norm": mix_norm[l][None, :],
            "s5_m": s5_m, "s5_bst": s5_bst, "s5_cst": s5_cst, "s5_av": s5_av,
            "s5_wglu": s5_w_glu[l].astype(bf16), "s5_bglu": s5_b_glu[l][None, :],
            "hg_lb": lbs[l][None, :], "hg_norm": hg_norm[l][None, :],
            "ml_gb": gbias, "ml_cw": _pad_heads(ml_conv_w[l], 1), "ml_cb": _pad_heads(ml_conv_b[l][None, :], 1),
            "ml_wqk": jnp.concatenate([wq, wk], axis=-1).astype(bf16),
            "ml_norm": _pad_heads(ml_norm[l][None, :], 1),
            "wb_a": wb[:S5_WIDTH].astype(bf16), "wb_b": wb[S5_WIDTH:S5_WIDTH + HG_WIDTH].astype(bf16),
            "wb_c": _pad_heads(wb[S5_WIDTH + HG_WIDTH:], 0).astype(bf16),
            "ffn_norm": ffn_norm[l][None, :], "ffn_cv": ffn_cv,
            "final_norm": final_norm[None, :],
        }
        xs = _layer(xs, p, big, l, nb, seq, tiles, l == depth - 1)
    return xs.reshape(nb, seq, d).astype(x.dtype)
```

```python
import functools

import jax
import jax.numpy as jnp
from jax import lax
from jax.experimental import pallas as pl
from jax.experimental.pallas import tpu as pltpu

f32 = jnp.float32
bf16 = jnp.bfloat16

D_MODEL = 2048
EPS = 1e-6
CHUNK = 64

S5_GROUPS = 32
S5_P = 16
S5_WIDTH = 512
S5_STATE = 64
S5_L = 8
S5_OCT = 8

HG_HEADS = 6
HG_D = 128
HG_WIDTH = 768
HG_SUB = 8
HG_PER = 3

ML_HEADS = 4
ML_DH = 192
ML_DP = 256
ML_WIDTH = 768
ML_WP = ML_HEADS * ML_DP
ML_CONV = 4
ML_PER = 2

FFN_DIM = 5632
FFN_CONV = 3
FFN_TN = 512
FFN_NJ = FFN_DIM // FFN_TN

OFF_U, OFF_Q, OFF_F, OFF_I, OFF_OG = 0, 512, 1280, 2048, 2816
OFF_CX, OFF_V, OFF_O, OFF_IGFG = 3584, 4608, 5632, 6656
MIXER_COLS = 7168
GATE_COLS = 3 * D_MODEL

VMEM_LIMIT = 60 * 1024 * 1024


def _split3(x):
    hi = x.astype(bf16)
    r1 = x - hi.astype(f32)
    mid = r1.astype(bf16)
    lo = (r1 - mid.astype(f32)).astype(bf16)
    return hi, mid, lo


def _dot(a, b):
    return jnp.dot(a, b, preferred_element_type=f32)


def _dot_nt(a, b):
    return lax.dot_general(a, b, (((1,), (1,)), ((), ())), preferred_element_type=f32)


def _dot_tn(a, b):
    return lax.dot_general(a, b, (((0,), (0,)), ((), ())), preferred_element_type=f32)


def _exact_dot(a_bf16, x):
    hi, mid, lo = _split3(x)
    return _dot(a_bf16, hi) + _dot(a_bf16, mid) + _dot(a_bf16, lo)


def _rms(x, g):
    ms = jnp.mean(x * x, axis=-1, keepdims=True)
    return x * lax.rsqrt(ms + EPS) * g


def _norm_proj_kernel(x_ref, g_ref, w_ref, o_ref, hs_ref, *, act):
    @pl.when(pl.program_id(1) == 0)
    def _():
        hs_ref[...] = _rms(x_ref[...], g_ref[...]).astype(bf16)

    y = _dot(hs_ref[...], w_ref[...])
    if act == "sigmoid":
        y = jax.nn.sigmoid(y)
    o_ref[...] = y.astype(o_ref.dtype)


def _norm_proj(x, g, w, layer, *, act, out_dtype, tm, tn):
    t, d = x.shape
    n = w.shape[2]
    return pl.pallas_call(
        functools.partial(_norm_proj_kernel, act=act),
        grid=(t // tm, n // tn),
        in_specs=[
            pl.BlockSpec((tm, d), lambda i, j: (i, 0)),
            pl.BlockSpec((1, d), lambda i, j: (0, 0)),
            pl.BlockSpec((None, d, tn), lambda i, j: (layer, 0, j)),
        ],
        out_specs=pl.BlockSpec((tm, tn), lambda i, j: (i, j)),
        out_shape=jax.ShapeDtypeStruct((t, n), out_dtype),
        scratch_shapes=[pltpu.VMEM((tm, d), bf16)],
        compiler_params=pltpu.CompilerParams(dimension_semantics=("parallel", "arbitrary"),
                                             vmem_limit_bytes=VMEM_LIMIT),
        name="norm_proj_" + act,
    )(x, g, w)


def _s5_kernel(u_ref, m_ref, bst_ref, cst_ref, av_ref, y_ref, sin_ref, xp_ref):
    nc = u_ref.shape[0] // S5_L
    half = S5_OCT * S5_STATE
    xb = jnp.concatenate([u_ref[pl.ds(t, nc, stride=S5_L), :] for t in range(S5_L)], axis=1).astype(bf16)
    sin_ref[...] = _dot(xb, bst_ref[0])
    ar = av_ref[0, 0:1, :]
    ai = av_ref[0, 1:2, :]

    def step(c, carry):
        xr, xi = carry
        xp_ref[pl.ds(c, 1), :] = jnp.concatenate([xr, xi], axis=1)
        s = sin_ref[pl.ds(c, 1), :]
        return ar * xr - ai * xi + s[:, :half], ar * xi + ai * xr + s[:, half:]

    z = jnp.zeros((1, half), f32)
    lax.fori_loop(0, nc, step, (z, z), unroll=8)
    y = jax.nn.gelu(_dot(xb, m_ref[0]) + _dot(xp_ref[...].astype(bf16), cst_ref[0]))
    for t in range(S5_L):
        y_ref[pl.ds(t, nc, stride=S5_L), :] = y[:, t * 128:(t + 1) * 128]


def _s5_core(proj, m, bst, cst, av, layer, nb):
    t = proj.shape[0]
    seq = t // nb
    nc = seq // S5_L
    noct = S5_GROUPS // S5_OCT
    cols = S5_L * 128
    st = 2 * S5_OCT * S5_STATE
    return pl.pallas_call(
        _s5_kernel,
        grid=(noct, nb),
        in_specs=[
            pl.BlockSpec((seq, 128), lambda o, b: (b, OFF_U // 128 + o)),
            pl.BlockSpec((None, 1, cols, cols), lambda o, b: (layer, o, 0, 0)),
            pl.BlockSpec((None, 1, cols, st), lambda o, b: (layer, o, 0, 0)),
            pl.BlockSpec((None, 1, st, cols), lambda o, b: (layer, o, 0, 0)),
            pl.BlockSpec((None, 1, 8, st // 2), lambda o, b: (layer, o, 0, 0)),
        ],
        out_specs=pl.BlockSpec((seq, 128), lambda o, b: (b, o)),
        out_shape=jax.ShapeDtypeStruct((t, S5_WIDTH), f32),
        scratch_shapes=[pltpu.VMEM((nc, st), f32), pltpu.VMEM((nc, st), f32)],
        compiler_params=pltpu.CompilerParams(dimension_semantics=("parallel", "parallel")),
        name="s5_core",
    )(proj, m, bst, cst, av)


def _s5_prep(lam_re, lam_im, log_dt, b_re, b_im, c_re, c_im, d_skip):
    hp = lax.Precision.HIGHEST
    g, n, p, ln, oc = S5_GROUPS, S5_STATE, S5_P, S5_L, S5_OCT
    no = g // oc
    dt = jnp.exp(log_dt)[:, None]
    lr, li = lam_re, lam_im
    mag = jnp.exp(lr * dt)
    ar, ai = mag * jnp.cos(li * dt), mag * jnp.sin(li * dt)
    den = lr * lr + li * li
    cr = ((ar - 1.0) * lr + ai * li) / den
    ci = (ai * lr - (ar - 1.0) * li) / den
    bbr = cr[..., None] * b_re - ci[..., None] * b_im
    bbi = cr[..., None] * b_im + ci[..., None] * b_re
    tau = jnp.arange(ln + 1, dtype=f32)[:, None, None]
    pmag = jnp.exp(tau * (lr * dt)[None])
    pr, pi = pmag * jnp.cos(tau * (li * dt)[None]), pmag * jnp.sin(tau * (li * dt)[None])
    cpr = c_re[None] * pr[:, :, None, :] - c_im[None] * pi[:, :, None, :]
    cpi = c_re[None] * pi[:, :, None, :] + c_im[None] * pr[:, :, None, :]
    kern = (jnp.einsum("tgpn,gnq->tgpq", cpr[:ln], bbr, precision=hp)
            - jnp.einsum("tgpn,gnq->tgpq", cpi[:ln], bbi, precision=hp))
    dskip = d_skip.reshape(g, p)
    kern = kern.at[0].add(jnp.eye(p, dtype=f32)[None] * dskip[:, :, None])
    kern = jnp.concatenate([kern, jnp.zeros((1, g, p, p), f32)], axis=0)
    eye = jnp.eye(oc, dtype=bf16)
    blk = kern.reshape(ln + 1, no, oc, p, p).transpose(0, 1, 2, 4, 3)
    blk = (blk.astype(bf16)[:, :, :, :, None, :] * eye[None, None, :, None, :, None]).reshape(ln + 1, no, oc * p, oc * p)
    s_idx = jnp.arange(ln)[:, None]
    t_idx = jnp.arange(ln)[None, :]
    lag = jnp.where(t_idx >= s_idx, t_idx - s_idx, ln)
    m8 = blk[lag].transpose(2, 0, 3, 1, 4).reshape(no, ln * oc * p, ln * oc * p)
    tau_b = (ln - 1) - tau[:ln]
    bmag = jnp.exp(tau_b * (lr * dt)[None])
    prs, pis = bmag * jnp.cos(tau_b * (li * dt)[None]), bmag * jnp.sin(tau_b * (li * dt)[None])
    bbrt, bbit = bbr.transpose(0, 2, 1), bbi.transpose(0, 2, 1)
    bst_r = prs[:, :, None, :] * bbrt[None] - pis[:, :, None, :] * bbit[None]
    bst_i = prs[:, :, None, :] * bbit[None] + pis[:, :, None, :] * bbrt[None]

    def expand_b(x):
        x = x.astype(bf16).reshape(ln, no, oc, p, n)
        return (x[:, :, :, :, None, :] * eye[None, None, :, None, :, None]).reshape(ln, no, oc * p, oc * n)

    bst8 = jnp.concatenate([expand_b(bst_r), expand_b(bst_i)], axis=-1)
    bst8 = bst8.transpose(1, 0, 2, 3).reshape(no, ln * oc * p, 2 * oc * n)
    cst = jnp.stack([cpr[1:], -cpi[1:]], axis=0)
    cst = cst.astype(bf16).reshape(2, ln, no, oc, p, n).transpose(0, 1, 2, 3, 5, 4)
    cst8 = cst[:, :, :, :, :, None, :] * eye[None, None, None, :, None, :, None]
    cst8 = cst8.reshape(2, ln, no, oc * n, oc * p).transpose(2, 0, 3, 1, 4).reshape(no, 2 * oc * n, ln * oc * p)
    av = jnp.stack([pr[ln].reshape(no, oc * n), pi[ln].reshape(no, oc * n)], axis=1)
    av = jnp.concatenate([av, jnp.zeros((no, 6, oc * n), f32)], axis=1)
    return m8, bst8, cst8, av


def _hg_consts():
    row = lax.broadcasted_iota(jnp.int32, (CHUNK, CHUNK), 0)
    col = lax.broadcasted_iota(jnp.int32, (CHUNK, CHUNK), 1)
    diag_mask = ((row // HG_SUB) == (col // HG_SUB)) & (col <= row)
    levels = []
    half = HG_SUB
    while half < CHUNK:
        levels.append((half, ((row // (2 * half)) == (col // (2 * half)))
                       & ((row & half) != 0) & ((col & half) == 0)))
        half *= 2
    wrow = lax.broadcasted_iota(jnp.int32, (HG_SUB * HG_D, CHUNK), 0)
    wcol = lax.broadcasted_iota(jnp.int32, (HG_SUB * HG_D, CHUNK), 1)
    wsel = ((wcol % HG_SUB) == (wrow // HG_D)).astype(bf16)
    return diag_mask, levels, wsel


def _hg_prelude(f_ref, lb, tril, k_s, cum_s):
    tb = f_ref.shape[0]
    f = lb + (1.0 - lb) * jax.nn.sigmoid(f_ref[...])
    k_s[...] = 1.0 - f
    logf = jnp.log2(f)
    for c in range(tb // CHUNK):
        cum_s[c * CHUNK:(c + 1) * CHUNK, :] = _exact_dot(tril, logf[c * CHUNK:(c + 1) * CHUNK])


def _hg_chunk(c, q_ref, i_ref, og_ref, ng, y_ref, lanes, k_s, cum_s, st, consts):
    diag_mask, levels, wsel = consts
    nblk = CHUNK // HG_SUB
    r0 = c * CHUNK
    q = jax.nn.silu(q_ref[r0:r0 + CHUNK, :])
    k = k_s[r0:r0 + CHUNK, :]
    cum = cum_s[r0:r0 + CHUNK, :]
    vb = i_ref[r0:r0 + CHUNK, :].astype(bf16)
    last = cum_s[r0 + CHUNK - 1:r0 + CHUNK, :]
    inter = _dot_nt((q * jnp.exp2(cum)).astype(bf16), st.astype(bf16))
    slabs = []
    for s in range(HG_SUB):
        pieces = []
        for b in range(nblk):
            rr = r0 + b * HG_SUB + s
            blk = slice(b * HG_SUB, (b + 1) * HG_SUB)
            dec = jnp.exp2(jnp.minimum(cum[blk] - cum_s[rr:rr + 1, :], 0.0))
            pieces.append((q[blk] * k_s[rr:rr + 1, :]) * dec)
        slabs.append(jnp.concatenate(pieces, axis=0).astype(bf16))
    sc = jnp.where(diag_mask, _dot(jnp.concatenate(slabs, axis=1), wsel), 0.0)
    for half, mask in levels:
        zs = []
        for base in range(0, CHUNK, 2 * half):
            mid = cum_s[r0 + base + half - 1:r0 + base + half, :]
            lo, hi = slice(base, base + half), slice(base + half, base + 2 * half)
            zs.append(k[lo] * jnp.exp2(mid - cum[lo]))
            zs.append(q[hi] * jnp.exp2(cum[hi] - mid))
        z = jnp.concatenate(zs, axis=0).astype(bf16)
        sc = jnp.where(mask, _dot_nt(z, z), sc)
    out = inter + _dot(sc.astype(bf16), vb)
    kdec = (k * jnp.exp2(last - cum)).astype(bf16)
    st = st * jnp.exp2(last) + _dot_tn(vb, kdec)
    o = _rms(out, ng) * jax.nn.silu(og_ref[r0:r0 + CHUNK, :])
    y_ref[r0:r0 + CHUNK, lanes] = o.astype(y_ref.dtype)
    return st


def _ml_prelude(cx_ref, lanes, gt_ref, gb, head, cw, cb, wqk, tril, carry_ref, buf_ref, q_s, k_s):
    tb = gt_ref.shape[0]
    nch = tb // CHUNK
    lane = lax.broadcasted_iota(jnp.int32, (CHUNK, 128), 1)
    lane_b = lax.broadcasted_iota(jnp.int32, (tb, 128), 1)
    gts = gt_ref[...] + gb
    ib_all = jnp.sum(jnp.where(lane_b == head, gts, 0.0), axis=-1, keepdims=True)
    lf_all = jax.nn.log_sigmoid(jnp.sum(jnp.where(lane_b == head + ML_HEADS, gts, 0.0), axis=-1, keepdims=True))
    lfc = jnp.zeros((CHUNK, 128), f32)
    ibc = jnp.zeros((CHUNK, 128), f32)
    for c in range(nch):
        lfc = jnp.where(lane == c, lf_all[c * CHUNK:(c + 1) * CHUNK], lfc)
        ibc = jnp.where(lane == c, ib_all[c * CHUNK:(c + 1) * CHUNK], ibc)
    bc = _exact_dot(tril, lfc)
    zt = (ibc - bc).T
    cx = cx_ref[:, lanes]
    buf_ref[0:8, :] = carry_ref[...]
    buf_ref[8:, :] = cx
    carry_ref[...] = cx[tb - 8:, :]
    ca = cw[ML_CONV - 1:ML_CONV, :] * cx + cb
    for kk in range(ML_CONV - 1):
        sh = ML_CONV - 1 - kk
        ca = ca + cw[kk:kk + 1, :] * buf_ref[pl.ds(8 - sh, tb), :]
    qk = _dot(jax.nn.silu(ca).astype(bf16), wqk)
    q_s[...] = qk[:, :ML_DP].astype(bf16)
    k_s[...] = (qk[:, ML_DP:] * (ML_DH ** -0.5)).astype(bf16)
    return ib_all, bc, zt


def _ml_chunk(c, gates, v_ref, o_ref, lanes, ng, y_ref, q_s, k_s, cmat, m_prev):
    ib_all, bc, zt = gates
    row = lax.broadcasted_iota(jnp.int32, (CHUNK, CHUNK), 0)
    col = lax.broadcasted_iota(jnp.int32, (CHUNK, CHUNK), 1)
    lane_w = lax.broadcasted_iota(jnp.int32, (CHUNK, ML_DP), 1)
    r0 = c * CHUNK
    ib = ib_all[r0:r0 + CHUNK]
    bcum = bc[:, c:c + 1]
    dmat = jnp.where(row >= col, bcum + zt[c:c + 1, :], -jnp.inf)
    m_t = jnp.maximum(bcum + m_prev, jnp.max(dmat, axis=-1, keepdims=True))
    m_new = m_t[CHUNK - 1:CHUNK, :]
    b_last = bcum[CHUNK - 1:CHUNK, :]
    q = q_s[r0:r0 + CHUNK, :]
    k = k_s[r0:r0 + CHUNK, :]
    vaug = jnp.where(lane_w == ML_DH, 1.0, v_ref[r0:r0 + CHUNK, lanes]).astype(bf16)
    w_inter = jnp.exp(bcum + m_prev - m_t)
    w_intra = jnp.exp(dmat - m_t) * _dot_nt(q, k)
    num = w_inter * _dot(q, cmat.astype(bf16)) + _dot(w_intra.astype(bf16), vaug)
    decay = jnp.exp(b_last + m_prev - m_new)
    ws = jnp.exp(b_last - bcum + ib - m_new)
    cmat = decay * cmat + _dot_tn((ws * k.astype(f32)).astype(bf16), vaug)
    den = jnp.sum(jnp.where(lane_w == ML_DH, num, 0.0), axis=-1, keepdims=True)
    h = num / jnp.maximum(jnp.abs(den), jnp.exp(-m_t))
    hm = jnp.where(lane_w < ML_DH, h, 0.0)
    ms = jnp.sum(hm * hm, axis=-1, keepdims=True) * (1.0 / ML_DH)
    hn = hm * lax.rsqrt(ms + EPS) * ng
    y_ref[r0:r0 + CHUNK, lanes] = (hn * jax.nn.sigmoid(o_ref[r0:r0 + CHUNK, lanes])).astype(y_ref.dtype)
    return cmat, m_new


def _mixers_kernel(*refs):
    hg_in = refs[:4 * HG_PER]
    lb_ref, hng_ref, cx_ref, v_ref, o_ref, gt_ref, gb_ref, cw_ref, cb_ref, wqk_ref, mng_ref = refs[4 * HG_PER:4 * HG_PER + 11]
    yb_ref, yc_ref = refs[4 * HG_PER + 11:4 * HG_PER + 13]
    st_ref, hk_s, cum_s, c_ref, m_ref, carry_ref, buf_ref, q_s, k_s = refs[4 * HG_PER + 13:]
    tb = gt_ref.shape[0]
    group = pl.program_id(1)

    @pl.when(pl.program_id(2) == 0)
    def _():
        st_ref[...] = jnp.zeros_like(st_ref)
        c_ref[...] = jnp.zeros_like(c_ref)
        m_ref[...] = jnp.zeros_like(m_ref)
        carry_ref[...] = jnp.zeros_like(carry_ref)

    row = lax.broadcasted_iota(jnp.int32, (CHUNK, CHUNK), 0)
    col = lax.broadcasted_iota(jnp.int32, (CHUNK, CHUNK), 1)
    tril = (row >= col).astype(bf16)
    consts = _hg_consts()
    gb = gb_ref[...]
    hg_lanes = [slice(h * HG_D, (h + 1) * HG_D) for h in range(HG_PER)]
    ml_lanes = [slice(h * ML_DP, (h + 1) * ML_DP) for h in range(ML_PER)]

    for h in range(HG_PER):
        _hg_prelude(hg_in[4 * h + 1], lb_ref[:, hg_lanes[h]], tril, hk_s.at[h], cum_s.at[h])
    gates = []
    for h in range(ML_PER):
        gates.append(_ml_prelude(cx_ref, ml_lanes[h], gt_ref, gb, group * ML_PER + h, cw_ref[:, ml_lanes[h]],
                                 cb_ref[:, ml_lanes[h]], wqk_ref[h], tril, carry_ref.at[h], buf_ref.at[h],
                                 q_s.at[h], k_s.at[h]))

    sts = [st_ref[h] for h in range(HG_PER)]
    cms = [c_ref[h] for h in range(ML_PER)]
    mps = [m_ref[h][0:1, 0:1] for h in range(ML_PER)]
    for c in range(tb // CHUNK):
        for h in range(max(HG_PER, ML_PER)):
            if h < HG_PER:
                sts[h] = _hg_chunk(c, hg_in[4 * h], hg_in[4 * h + 2], hg_in[4 * h + 3], hng_ref[:, hg_lanes[h]], yb_ref,
                                   hg_lanes[h], hk_s.at[h], cum_s.at[h], sts[h], consts)
            if h < ML_PER:
                cms[h], mps[h] = _ml_chunk(c, gates[h], v_ref, o_ref, ml_lanes[h], mng_ref[:, ml_lanes[h]], yc_ref,
                                           q_s.at[h], k_s.at[h], cms[h], mps[h])
    for h in range(HG_PER):
        st_ref[h] = sts[h]
    for h in range(ML_PER):
        c_ref[h] = cms[h]
        m_ref[h] = jnp.broadcast_to(mps[h], m_ref.shape[1:])


def _mixers(proj, lb, hng, gb, cw, cb, wqk, mng, nb, seq, tb):
    t = proj.shape[0]
    nj = seq // tb
    hg_w, ml_w = HG_PER * HG_D, ML_PER * ML_DP

    def hseg(off, h):
        return pl.BlockSpec((tb, HG_D), lambda b, g, j, o=off // HG_D + h: (b * nj + j, o + HG_PER * g))

    def mseg(off):
        return pl.BlockSpec((tb, ml_w), lambda b, g, j, o=off // ml_w: (b * nj + j, o + g))

    in_specs = []
    for h in range(HG_PER):
        in_specs += [hseg(OFF_Q, h), hseg(OFF_F, h), hseg(OFF_I, h), hseg(OFF_OG, h)]
    in_specs += [
        pl.BlockSpec((1, hg_w), lambda b, g, j: (0, g)),
        pl.BlockSpec((1, hg_w), lambda b, g, j: (0, g)),
        mseg(OFF_CX), mseg(OFF_V), mseg(OFF_O),
        pl.BlockSpec((tb, 128), lambda b, g, j: (b * nj + j, OFF_IGFG // 128)),
        pl.BlockSpec((1, 128), lambda b, g, j: (0, 0)),
        pl.BlockSpec((ML_CONV, ml_w), lambda b, g, j: (0, g)),
        pl.BlockSpec((1, ml_w), lambda b, g, j: (0, g)),
        pl.BlockSpec((ML_PER, ML_DP, 2 * ML_DP), lambda b, g, j: (g, 0, 0)),
        pl.BlockSpec((1, ml_w), lambda b, g, j: (0, g)),
    ]
    return pl.pallas_call(
        _mixers_kernel,
        grid=(nb, HG_HEADS // HG_PER, nj),
        in_specs=in_specs,
        out_specs=[pl.BlockSpec((tb, hg_w), lambda b, g, j: (b * nj + j, g)),
                   pl.BlockSpec((tb, ml_w), lambda b, g, j: (b * nj + j, g))],
        out_shape=[jax.ShapeDtypeStruct((t, HG_WIDTH), bf16), jax.ShapeDtypeStruct((t, ML_WP), bf16)],
        scratch_shapes=[
            pltpu.VMEM((HG_PER, HG_D, HG_D), f32),
            pltpu.VMEM((HG_PER, tb, HG_D), f32),
            pltpu.VMEM((HG_PER, tb, HG_D), f32),
            pltpu.VMEM((ML_PER, ML_DP, ML_DP), f32),
            pltpu.VMEM((ML_PER, 8, 128), f32),
            pltpu.VMEM((ML_PER, 8, ML_DP), f32),
            pltpu.VMEM((ML_PER, tb + 8, ML_DP), f32),
            pltpu.VMEM((ML_PER, tb, ML_DP), bf16),
            pltpu.VMEM((ML_PER, tb, ML_DP), bf16),
        ],
        compiler_params=pltpu.CompilerParams(dimension_semantics=("parallel", "parallel", "arbitrary")),
        name="mixers",
    )(*([proj] * (4 * HG_PER)), lb, hng, proj, proj, proj, proj, gb, cw, cb, wqk, mng)


def _merge_kernel(g5_ref, yb_ref, yc_ref, sa_ref, sb_ref, sc_ref, wglu_ref, bglu_ref,
                  wa_ref, wb_ref, wc_ref, o_ref):
    g5 = g5_ref[...]
    ya = g5 * jax.nn.sigmoid(_dot(g5.astype(bf16), wglu_ref[...]) + bglu_ref[...])
    m = sa_ref[...].astype(f32) * _dot(ya.astype(bf16), wa_ref[...])
    m = m + sb_ref[...].astype(f32) * _dot(yb_ref[...], wb_ref[...])
    m = m + sc_ref[...].astype(f32) * _dot(yc_ref[...], wc_ref[...])
    o_ref[...] = m.astype(o_ref.dtype)


def _merge(g5, yb, yc, sg, wglu, bglu, wa, wb, wc, tm):
    t = g5.shape[0]
    d = D_MODEL

    def rows(w):
        return pl.BlockSpec((tm, w), lambda i: (i, 0))

    def full(a):
        return pl.BlockSpec(a.shape, lambda i: (0, 0))

    return pl.pallas_call(
        _merge_kernel,
        grid=(t // tm,),
        in_specs=[
            rows(S5_WIDTH), rows(HG_WIDTH), rows(ML_WP),
            pl.BlockSpec((tm, d), lambda i: (i, 0)),
            pl.BlockSpec((tm, d), lambda i: (i, 1)),
            pl.BlockSpec((tm, d), lambda i: (i, 2)),
            full(wglu), full(bglu), full(wa), full(wb), full(wc),
        ],
        out_specs=rows(d),
        out_shape=jax.ShapeDtypeStruct((t, d), bf16),
        compiler_params=pltpu.CompilerParams(dimension_semantics=("parallel",)),
        name="merge",
    )(g5, yb, yc, sg, sg, sg, wglu, bglu, wa, wb, wc)


def _proj_res_kernel(a_ref, w_ref, r_ref, o_ref):
    o_ref[...] = r_ref[...] + _dot(a_ref[...], w_ref[...])


def _proj_res(a, w, layer, res, tm):
    t, k = a.shape
    n = w.shape[2]
    return pl.pallas_call(
        _proj_res_kernel,
        grid=(t // tm,),
        in_specs=[
            pl.BlockSpec((tm, k), lambda i: (i, 0)),
            pl.BlockSpec((None, k, n), lambda i: (layer, 0, 0)),
            pl.BlockSpec((tm, n), lambda i: (i, 0)),
        ],
        out_specs=pl.BlockSpec((tm, n), lambda i: (i, 0)),
        out_shape=jax.ShapeDtypeStruct((t, n), f32),
        compiler_params=pltpu.CompilerParams(dimension_semantics=("parallel",)),
        name="proj_res",
    )(a, w, res)


def _ffn_kernel(x_ref, g_ref, wa_ref, wb_ref, cva_ref, cvb_ref, wd_ref, fg_ref, o_ref,
                hs_ref, bufa_ref, bufb_ref, cara_ref, carb_ref, *, tiles_per_seq, final):
    tm = x_ref.shape[0]
    i = pl.program_id(0)
    j = pl.program_id(1)

    @pl.when(j == 0)
    def _():
        x = x_ref[...]
        hs_ref[...] = _rms(x, g_ref[...]).astype(bf16)
        o_ref[...] = x

    @pl.when(i % tiles_per_seq == 0)
    def _():
        cara_ref[j] = jnp.zeros((8, FFN_TN), f32)
        carb_ref[j] = jnp.zeros((8, FFN_TN), f32)

    hs = hs_ref[...]

    def conv(w_ref, cv_ref, buf_ref, car_ref):
        u = _dot(hs, w_ref[...])
        buf_ref[0:8, :] = car_ref[j]
        buf_ref[8:, :] = u
        car_ref[j] = u[tm - 8:, :]
        y = cv_ref[FFN_CONV - 1:FFN_CONV, :] * u + cv_ref[FFN_CONV:FFN_CONV + 1, :]
        for kk in range(FFN_CONV - 1):
            sh = FFN_CONV - 1 - kk
            y = y + cv_ref[kk:kk + 1, :] * buf_ref[pl.ds(8 - sh, tm), :]
        return y

    a = conv(wa_ref, cva_ref, bufa_ref, cara_ref)
    b = conv(wb_ref, cvb_ref, bufb_ref, carb_ref)
    act = (jax.nn.silu(a) * b).astype(bf16)
    o_ref[...] += _dot(act, wd_ref[...])

    if final:
        @pl.when(j == FFN_NJ - 1)
        def _():
            o_ref[...] = _rms(o_ref[...], fg_ref[...])


def _ffn(x, g, wup, cv, wd, layer, fg, seq, tm, final):
    t, d = x.shape
    return pl.pallas_call(
        functools.partial(_ffn_kernel, tiles_per_seq=seq // tm, final=final),
        grid=(t // tm, FFN_NJ),
        in_specs=[
            pl.BlockSpec((tm, d), lambda i, j: (i, 0), pipeline_mode=pl.Buffered(1)),
            pl.BlockSpec((1, d), lambda i, j: (0, 0)),
            pl.BlockSpec((None, d, FFN_TN), lambda i, j: (layer, 0, j)),
            pl.BlockSpec((None, d, FFN_TN), lambda i, j: (layer, 0, j + FFN_NJ)),
            pl.BlockSpec((8, FFN_TN), lambda i, j: (0, j)),
            pl.BlockSpec((8, FFN_TN), lambda i, j: (0, j + FFN_NJ)),
            pl.BlockSpec((None, FFN_TN, d), lambda i, j: (layer, j, 0)),
            pl.BlockSpec((1, d), lambda i, j: (0, 0)),
        ],
        out_specs=pl.BlockSpec((tm, d), lambda i, j: (i, 0)),
        out_shape=jax.ShapeDtypeStruct((t, d), f32),
        scratch_shapes=[
            pltpu.VMEM((tm, d), bf16),
            pltpu.VMEM((tm + 8, FFN_TN), f32),
            pltpu.VMEM((tm + 8, FFN_TN), f32),
            pltpu.VMEM((FFN_NJ, 8, FFN_TN), f32),
            pltpu.VMEM((FFN_NJ, 8, FFN_TN), f32),
        ],
        compiler_params=pltpu.CompilerParams(dimension_semantics=("arbitrary", "arbitrary"),
                                             vmem_limit_bytes=VMEM_LIMIT),
        name="ffn",
    )(x, g, wup, wup, cv, cv, wd, fg)


def _pad_heads(w, axis):
    shp = w.shape
    w = w.reshape(shp[:axis] + (ML_HEADS, ML_DH) + shp[axis + 1:])
    pad = [(0, 0)] * w.ndim
    pad[axis + 1] = (0, ML_DP - ML_DH)
    w = jnp.pad(w, pad)
    return w.reshape(shp[:axis] + (ML_WP,) + shp[axis + 1:])


def _pack_w_in_kernel(w_ref, mix_ref, gate_ref):
    nproj = OFF_CX
    mix_ref[:, 0:nproj] = w_ref[:, 0:nproj].astype(bf16)
    for seg in range(3):
        for h in range(ML_HEADS):
            src = nproj + seg * ML_WIDTH + h * ML_DH
            dst = nproj + seg * ML_WP + h * ML_DP
            mix_ref[:, dst:dst + ML_DH] = w_ref[:, src:src + ML_DH].astype(bf16)
            mix_ref[:, dst + ML_DH:dst + ML_DP] = jnp.zeros((mix_ref.shape[0], ML_DP - ML_DH), bf16)
    src = nproj + 3 * ML_WIDTH
    mix_ref[:, OFF_IGFG:OFF_IGFG + 128] = jnp.zeros((mix_ref.shape[0], 128), bf16)
    mix_ref[:, OFF_IGFG:OFF_IGFG + 2 * ML_HEADS] = w_ref[:, src:src + 2 * ML_HEADS].astype(bf16)
    mix_ref[:, OFF_IGFG + 128:] = jnp.zeros((mix_ref.shape[0], MIXER_COLS - OFF_IGFG - 128), bf16)
    gate_ref[...] = w_ref[:, src + 2 * ML_HEADS:].astype(bf16)


def _pack_w_in(w, tr):
    nl, d, n = w.shape
    return pl.pallas_call(
        _pack_w_in_kernel,
        grid=(nl, d // tr),
        in_specs=[pl.BlockSpec((None, tr, n), lambda l, i: (l, i, 0))],
        out_specs=[pl.BlockSpec((None, tr, MIXER_COLS), lambda l, i: (l, i, 0)),
                   pl.BlockSpec((None, tr, GATE_COLS), lambda l, i: (l, i, 0))],
        out_shape=[jax.ShapeDtypeStruct((nl, d, MIXER_COLS), bf16), jax.ShapeDtypeStruct((nl, d, GATE_COLS), bf16)],
        compiler_params=pltpu.CompilerParams(dimension_semantics=("parallel", "parallel")),
        name="pack_w_in",
    )(w)


def _layer(x, p, big, layer, nb, seq, tiles, last):
    proj = _norm_proj(x, p["mix_norm"], big["w_mixer"], layer, act="none", out_dtype=f32,
                      tm=tiles["tm_proj"], tn=tiles["tn_mixer"])
    sg = _norm_proj(x, p["mix_norm"], big["w_gates"], layer, act="sigmoid", out_dtype=bf16,
                    tm=tiles["tm_proj"], tn=tiles["tn_gates"])
    g5 = _s5_core(proj, big["s5_m"], big["s5_bst"], big["s5_cst"], big["s5_av"], layer, nb)
    yb, yc = _mixers(proj, p["hg_lb"], p["hg_norm"], p["ml_gb"], p["ml_cw"], p["ml_cb"], p["ml_wqk"], p["ml_norm"],
                     nb, seq, tiles["tb"])
    merged = _merge(g5, yb, yc, sg, p["s5_wglu"], p["s5_bglu"], p["wb_a"], p["wb_b"], p["wb_c"], tiles["tm_merge"])
    x = _proj_res(merged, big["w_out"], layer, x, tiles["tm_merge"])
    return _ffn(x, p["ffn_norm"], big["ffn_wup"], p["ffn_cv"], big["ffn_wd"], layer, p["final_norm"], seq,
                tiles["tm_ffn"], last)


def _tiles(seq):
    return {
        "tm_proj": min(1024, seq), "tn_mixer": MIXER_COLS // 4, "tn_gates": GATE_COLS // 3,
        "tb": min(512, seq), "tm_merge": min(512, seq), "tm_ffn": min(1024, seq), "tr_pack": 256,
    }


def kernel(x, mix_norm, w_in, s5_lam_re, s5_lam_im, s5_log_dt, s5_b_re, s5_b_im, s5_c_re, s5_c_im, s5_d, s5_w_glu, s5_b_glu, hg_lower_bounds, hg_norm, ml_conv_w, ml_conv_b, ml_w_qk, ml_b_ig, ml_b_fg, ml_norm, w_branch, w_out, ffn_norm, ffn_w_up, ffn_conv_w, ffn_conv_b, ffn_w_down, final_norm):
    nb, seq, d = x.shape
    depth = w_in.shape[0]
    tiles = _tiles(seq)
    lbs = jax.nn.softmax(hg_lower_bounds.astype(f32), axis=0)
    lbs = jnp.cumsum(lbs, axis=0) - lbs[0:1]
    xs = x.astype(f32).reshape(nb * seq, d)
    w_mixer, w_gates = _pack_w_in(w_in, tiles["tr_pack"])
    s5_m, s5_bst, s5_cst, s5_av = jax.vmap(_s5_prep)(
        s5_lam_re, s5_lam_im, s5_log_dt, s5_b_re, s5_b_im, s5_c_re, s5_c_im, s5_d)
    big = {"w_mixer": w_mixer, "w_gates": w_gates, "w_out": w_out.astype(bf16),
           "ffn_wup": ffn_w_up.astype(bf16), "ffn_wd": ffn_w_down.astype(bf16),
           "s5_m": s5_m, "s5_bst": s5_bst, "s5_cst": s5_cst, "s5_av": s5_av}
    for l in range(depth):
        wqk = ml_w_qk[l]
        wq = jnp.pad(wqk[:, :, :ML_DH], ((0, 0), (0, ML_DP - ML_DH), (0, ML_DP - ML_DH)))
        wk = jnp.pad(wqk[:, :, ML_DH:], ((0, 0), (0, ML_DP - ML_DH), (0, ML_DP - ML_DH)))
        gbias = jnp.concatenate([ml_b_ig[l], ml_b_fg[l], jnp.zeros((128 - 2 * ML_HEADS,), f32)])[None, :]
        wb = w_branch[l]
        ffn_cv = jnp.concatenate([ffn_conv_w[l], ffn_conv_b[l][None, :],
                                  jnp.zeros((8 - FFN_CONV - 1, 2 * FFN_DIM), f32)], axis=0)
        p = {
            "mix_norm": mix_norm[l][None, :],
            "s5_wglu": s5_w_glu[l].astype(bf16), "s5_bglu": s5_b_glu[l][None, :],
            "hg_lb": lbs[l][None, :], "hg_norm": hg_norm[l][None, :],
            "ml_gb": gbias, "ml_cw": _pad_heads(ml_conv_w[l], 1), "ml_cb": _pad_heads(ml_conv_b[l][None, :], 1),
            "ml_wqk": jnp.concatenate([wq, wk], axis=-1).astype(bf16),
            "ml_norm": _pad_heads(ml_norm[l][None, :], 1),
            "wb_a": wb[:S5_WIDTH].astype(bf16), "wb_b": wb[S5_WIDTH:S5_WIDTH + HG_WIDTH].astype(bf16),
            "wb_c": _pad_heads(wb[S5_WIDTH + HG_WIDTH:], 0).astype(bf16),
            "ffn_norm": ffn_norm[l][None, :], "ffn_cv": ffn_cv,
            "final_norm": final_norm[None, :],
        }
        xs = _layer(xs, p, big, l, nb, seq, tiles, l == depth - 1)
    return xs.reshape(nb, seq, d).astype(x.dtype)
```

```python
import functools

import jax
import jax.numpy as jnp
from jax import lax
from jax.experimental import pallas as pl
from jax.experimental.pallas import tpu as pltpu

f32 = jnp.float32
bf16 = jnp.bfloat16

D_MODEL = 2048
EPS = 1e-6
CHUNK = 64

S5_GROUPS = 32
S5_P = 16
S5_WIDTH = 512
S5_STATE = 64
S5_L = 8
S5_OCT = 8

HG_HEADS = 6
HG_D = 128
HG_WIDTH = 768
HG_SUB = 8
HG_PER = 3

ML_HEADS = 4
ML_DH = 192
ML_DP = 256
ML_WIDTH = 768
ML_WP = ML_HEADS * ML_DP
ML_CONV = 4
ML_PER = 2

FFN_DIM = 5632
FFN_CONV = 3
FFN_TN = 512
FFN_NJ = FFN_DIM // FFN_TN

OFF_U, OFF_Q, OFF_F, OFF_I, OFF_OG = 0, 512, 1280, 2048, 2816
OFF_CX, OFF_V, OFF_O, OFF_IGFG = 3584, 4608, 5632, 6656
MIXER_COLS = 7168
GATE_COLS = 3 * D_MODEL

VMEM_LIMIT = 60 * 1024 * 1024


def _split3(x):
    hi = x.astype(bf16)
    r1 = x - hi.astype(f32)
    mid = r1.astype(bf16)
    lo = (r1 - mid.astype(f32)).astype(bf16)
    return hi, mid, lo


def _dot(a, b):
    return jnp.dot(a, b, preferred_element_type=f32)


def _dot_nt(a, b):
    return lax.dot_general(a, b, (((1,), (1,)), ((), ())), preferred_element_type=f32)


def _dot_tn(a, b):
    return lax.dot_general(a, b, (((0,), (0,)), ((), ())), preferred_element_type=f32)


def _exact_dot(a_bf16, x):
    hi, mid, lo = _split3(x)
    return _dot(a_bf16, hi) + _dot(a_bf16, mid) + _dot(a_bf16, lo)


def _rms(x, g):
    ms = jnp.mean(x * x, axis=-1, keepdims=True)
    return x * lax.rsqrt(ms + EPS) * g


def _norm_proj_kernel(x_ref, g_ref, w_ref, o_ref, hs_ref, *, act):
    @pl.when(pl.program_id(1) == 0)
    def _():
        hs_ref[...] = _rms(x_ref[...], g_ref[...]).astype(bf16)

    y = _dot(hs_ref[...], w_ref[...])
    if act == "sigmoid":
        y = jax.nn.sigmoid(y)
    o_ref[...] = y.astype(o_ref.dtype)


def _norm_proj(x, g, w, layer, *, act, out_dtype, tm, tn):
    t, d = x.shape
    n = w.shape[2]
    return pl.pallas_call(
        functools.partial(_norm_proj_kernel, act=act),
        grid=(t // tm, n // tn),
        in_specs=[
            pl.BlockSpec((tm, d), lambda i, j: (i, 0)),
            pl.BlockSpec((1, d), lambda i, j: (0, 0)),
            pl.BlockSpec((None, d, tn), lambda i, j: (layer, 0, j)),
        ],
        out_specs=pl.BlockSpec((tm, tn), lambda i, j: (i, j)),
        out_shape=jax.ShapeDtypeStruct((t, n), out_dtype),
        scratch_shapes=[pltpu.VMEM((tm, d), bf16)],
        compiler_params=pltpu.CompilerParams(dimension_semantics=("parallel", "arbitrary"),
                                             vmem_limit_bytes=VMEM_LIMIT),
        name="norm_proj_" + act,
    )(x, g, w)


def _s5_kernel(u_ref, m_ref, bst_ref, cst_ref, av_ref, y_ref, sin_ref, xp_ref):
    nc = u_ref.shape[0] // S5_L
    half = S5_OCT * S5_STATE
    xb = jnp.concatenate([u_ref[pl.ds(t, nc, stride=S5_L), :] for t in range(S5_L)], axis=1).astype(bf16)
    sin_ref[...] = _dot(xb, bst_ref[0])
    ar = av_ref[0, 0:1, :]
    ai = av_ref[0, 1:2, :]

    def step(c, carry):
        xr, xi = carry
        xp_ref[pl.ds(c, 1), :] = jnp.concatenate([xr, xi], axis=1)
        s = sin_ref[pl.ds(c, 1), :]
        return ar * xr - ai * xi + s[:, :half], ar * xi + ai * xr + s[:, half:]

    z = jnp.zeros((1, half), f32)
    lax.fori_loop(0, nc, step, (z, z), unroll=8)
    y = jax.nn.gelu(_dot(xb, m_ref[0]) + _dot(xp_ref[...].astype(bf16), cst_ref[0]))
    for t in range(S5_L):
        y_ref[pl.ds(t, nc, stride=S5_L), :] = y[:, t * 128:(t + 1) * 128]


def _s5_core(proj, m, bst, cst, av, layer, nb):
    t = proj.shape[0]
    seq = t // nb
    nc = seq // S5_L
    noct = S5_GROUPS // S5_OCT
    cols = S5_L * 128
    st = 2 * S5_OCT * S5_STATE
    return pl.pallas_call(
        _s5_kernel,
        grid=(noct, nb),
        in_specs=[
            pl.BlockSpec((seq, 128), lambda o, b: (b, OFF_U // 128 + o)),
            pl.BlockSpec((None, 1, cols, cols), lambda o, b: (layer, o, 0, 0)),
            pl.BlockSpec((None, 1, cols, st), lambda o, b: (layer, o, 0, 0)),
            pl.BlockSpec((None, 1, st, cols), lambda o, b: (layer, o, 0, 0)),
            pl.BlockSpec((None, 1, 8, st // 2), lambda o, b: (layer, o, 0, 0)),
        ],
        out_specs=pl.BlockSpec((seq, 128), lambda o, b: (b, o)),
        out_shape=jax.ShapeDtypeStruct((t, S5_WIDTH), f32),
        scratch_shapes=[pltpu.VMEM((nc, st), f32), pltpu.VMEM((nc, st), f32)],
        compiler_params=pltpu.CompilerParams(dimension_semantics=("parallel", "parallel")),
        name="s5_core",
    )(proj, m, bst, cst, av)


def _s5_prep(lam_re, lam_im, log_dt, b_re, b_im, c_re, c_im, d_skip):
    hp = lax.Precision.HIGHEST
    g, n, p, ln, oc = S5_GROUPS, S5_STATE, S5_P, S5_L, S5_OCT
    no = g // oc
    dt = jnp.exp(log_dt)[:, None]
    lr, li = lam_re, lam_im
    mag = jnp.exp(lr * dt)
    ar, ai = mag * jnp.cos(li * dt), mag * jnp.sin(li * dt)
    den = lr * lr + li * li
    cr = ((ar - 1.0) * lr + ai * li) / den
    ci = (ai * lr - (ar - 1.0) * li) / den
    bbr = cr[..., None] * b_re - ci[..., None] * b_im
    bbi = cr[..., None] * b_im + ci[..., None] * b_re
    tau = jnp.arange(ln + 1, dtype=f32)[:, None, None]
    pmag = jnp.exp(tau * (lr * dt)[None])
    pr, pi = pmag * jnp.cos(tau * (li * dt)[None]), pmag * jnp.sin(tau * (li * dt)[None])
    cpr = c_re[None] * pr[:, :, None, :] - c_im[None] * pi[:, :, None, :]
    cpi = c_re[None] * pi[:, :, None, :] + c_im[None] * pr[:, :, None, :]
    kern = (jnp.einsum("tgpn,gnq->tgpq", cpr[:ln], bbr, precision=hp)
            - jnp.einsum("tgpn,gnq->tgpq", cpi[:ln], bbi, precision=hp))
    dskip = d_skip.reshape(g, p)
    kern = kern.at[0].add(jnp.eye(p, dtype=f32)[None] * dskip[:, :, None])
    kern = jnp.concatenate([kern, jnp.zeros((1, g, p, p), f32)], axis=0)
    eye = jnp.eye(oc, dtype=bf16)
    blk = kern.reshape(ln + 1, no, oc, p, p).transpose(0, 1, 2, 4, 3)
    blk = (blk.astype(bf16)[:, :, :, :, None, :] * eye[None, None, :, None, :, None]).reshape(ln + 1, no, oc * p, oc * p)
    s_idx = jnp.arange(ln)[:, None]
    t_idx = jnp.arange(ln)[None, :]
    lag = jnp.where(t_idx >= s_idx, t_idx - s_idx, ln)
    m8 = blk[lag].transpose(2, 0, 3, 1, 4).reshape(no, ln * oc * p, ln * oc * p)
    tau_b = (ln - 1) - tau[:ln]
    bmag = jnp.exp(tau_b * (lr * dt)[None])
    prs, pis = bmag * jnp.cos(tau_b * (li * dt)[None]), bmag * jnp.sin(tau_b * (li * dt)[None])
    bbrt, bbit = bbr.transpose(0, 2, 1), bbi.transpose(0, 2, 1)
    bst_r = prs[:, :, None, :] * bbrt[None] - pis[:, :, None, :] * bbit[None]
    bst_i = prs[:, :, None, :] * bbit[None] + pis[:, :, None, :] * bbrt[None]

    def expand_b(x):
        x = x.astype(bf16).reshape(ln, no, oc, p, n)
        return (x[:, :, :, :, None, :] * eye[None, None, :, None, :, None]).reshape(ln, no, oc * p, oc * n)

    bst8 = jnp.concatenate([expand_b(bst_r), expand_b(bst_i)], axis=-1)
    bst8 = bst8.transpose(1, 0, 2, 3).reshape(no, ln * oc * p, 2 * oc * n)
    cst = jnp.stack([cpr[1:], -cpi[1:]], axis=0)
    cst = cst.astype(bf16).reshape(2, ln, no, oc, p, n).transpose(0, 1, 2, 3, 5, 4)
    cst8 = cst[:, :, :, :, :, None, :] * eye[None, None, None, :, None, :, None]
    cst8 = cst8.reshape(2, ln, no, oc * n, oc * p).transpose(2, 0, 3, 1, 4).reshape(no, 2 * oc * n, ln * oc * p)
    av = jnp.stack([pr[ln].reshape(no, oc * n), pi[ln].reshape(no, oc * n)], axis=1)
    av = jnp.concatenate([av, jnp.zeros((no, 6, oc * n), f32)], axis=1)
    return m8, bst8, cst8, av


def _hg_consts():
    row = lax.broadcasted_iota(jnp.int32, (CHUNK, CHUNK), 0)
    col = lax.broadcasted_iota(jnp.int32, (CHUNK, CHUNK), 1)
    diag_mask = ((row // HG_SUB) == (col // HG_SUB)) & (col <= row)
    levels = []
    half = HG_SUB
    while half < CHUNK:
        levels.append((half, ((row // (2 * half)) == (col // (2 * half)))
                       & ((row & half) != 0) & ((col & half) == 0)))
        half *= 2
    wrow = lax.broadcasted_iota(jnp.int32, (HG_SUB * HG_D, CHUNK), 0)
    wcol = lax.broadcasted_iota(jnp.int32, (HG_SUB * HG_D, CHUNK), 1)
    wsel = ((wcol % HG_SUB) == (wrow // HG_D)).astype(bf16)
    return diag_mask, levels, wsel


def _hg_prelude(f_ref, lb, tril, k_s, cum_s):
    tb = f_ref.shape[0]
    f = lb + (1.0 - lb) * jax.nn.sigmoid(f_ref[...])
    k_s[...] = 1.0 - f
    logf = jnp.log2(f)
    for c in range(tb // CHUNK):
        cum_s[c * CHUNK:(c + 1) * CHUNK, :] = _exact_dot(tril, logf[c * CHUNK:(c + 1) * CHUNK])


def _hg_chunk(c, q_ref, i_ref, og_ref, ng, y_ref, lanes, k_s, cum_s, st, consts):
    diag_mask, levels, wsel = consts
    nblk = CHUNK // HG_SUB
    r0 = c * CHUNK
    q = jax.nn.silu(q_ref[r0:r0 + CHUNK, :])
    k = k_s[r0:r0 + CHUNK, :]
    cum = cum_s[r0:r0 + CHUNK, :]
    vb = i_ref[r0:r0 + CHUNK, :].astype(bf16)
    last = cum_s[r0 + CHUNK - 1:r0 + CHUNK, :]
    inter = _dot_nt((q * jnp.exp2(cum)).astype(bf16), st.astype(bf16))
    slabs = []
    for s in range(HG_SUB):
        pieces = []
        for b in range(nblk):
            rr = r0 + b * HG_SUB + s
            blk = slice(b * HG_SUB, (b + 1) * HG_SUB)
            dec = jnp.exp2(jnp.minimum(cum[blk] - cum_s[rr:rr + 1, :], 0.0))
            pieces.append((q[blk] * k_s[rr:rr + 1, :]) * dec)
        slabs.append(jnp.concatenate(pieces, axis=0).astype(bf16))
    sc = jnp.where(diag_mask, _dot(jnp.concatenate(slabs, axis=1), wsel), 0.0)
    for half, mask in levels:
        zs = []
        for base in range(0, CHUNK, 2 * half):
            mid = cum_s[r0 + base + half - 1:r0 + base + half, :]
            lo, hi = slice(base, base + half), slice(base + half, base + 2 * half)
            zs.append(k[lo] * jnp.exp2(mid - cum[lo]))
            zs.append(q[hi] * jnp.exp2(cum[hi] - mid))
        z = jnp.concatenate(zs, axis=0).astype(bf16)
        sc = jnp.where(mask, _dot_nt(z, z), sc)
    out = inter + _dot(sc.astype(bf16), vb)
    kdec = (k * jnp.exp2(last - cum)).astype(bf16)
    st = st * jnp.exp2(last) + _dot_tn(vb, kdec)
    o = _rms(out, ng) * jax.nn.silu(og_ref[r0:r0 + CHUNK, :])
    y_ref[r0:r0 + CHUNK, lanes] = o.astype(y_ref.dtype)
    return st


def _ml_prelude(cx_ref, lanes, gt_ref, gb, head, cw, cb, wqk, tril, carry_ref, buf_ref, q_s, k_s):
    tb = gt_ref.shape[0]
    nch = tb // CHUNK
    lane = lax.broadcasted_iota(jnp.int32, (CHUNK, 128), 1)
    lane_b = lax.broadcasted_iota(jnp.int32, (tb, 128), 1)
    gts = gt_ref[...] + gb
    ib_all = jnp.sum(jnp.where(lane_b == head, gts, 0.0), axis=-1, keepdims=True)
    lf_all = jax.nn.log_sigmoid(jnp.sum(jnp.where(lane_b == head + ML_HEADS, gts, 0.0), axis=-1, keepdims=True))
    lfc = jnp.zeros((CHUNK, 128), f32)
    ibc = jnp.zeros((CHUNK, 128), f32)
    for c in range(nch):
        lfc = jnp.where(lane == c, lf_all[c * CHUNK:(c + 1) * CHUNK], lfc)
        ibc = jnp.where(lane == c, ib_all[c * CHUNK:(c + 1) * CHUNK], ibc)
    bc = _exact_dot(tril, lfc)
    zt = (ibc - bc).T
    cx = cx_ref[:, lanes]
    buf_ref[0:8, :] = carry_ref[...]
    buf_ref[8:, :] = cx
    carry_ref[...] = cx[tb - 8:, :]
    ca = cw[ML_CONV - 1:ML_CONV, :] * cx + cb
    for kk in range(ML_CONV - 1):
        sh = ML_CONV - 1 - kk
        ca = ca + cw[kk:kk + 1, :] * buf_ref[pl.ds(8 - sh, tb), :]
    qk = _dot(jax.nn.silu(ca).astype(bf16), wqk)
    q_s[...] = qk[:, :ML_DP].astype(bf16)
    k_s[...] = (qk[:, ML_DP:] * (ML_DH ** -0.5)).astype(bf16)
    return ib_all, bc, zt


def _ml_chunk(c, gates, v_ref, o_ref, lanes, ng, y_ref, q_s, k_s, cmat, m_prev):
    ib_all, bc, zt = gates
    row = lax.broadcasted_iota(jnp.int32, (CHUNK, CHUNK), 0)
    col = lax.broadcasted_iota(jnp.int32, (CHUNK, CHUNK), 1)
    lane_w = lax.broadcasted_iota(jnp.int32, (CHUNK, ML_DP), 1)
    r0 = c * CHUNK
    ib = ib_all[r0:r0 + CHUNK]
    bcum = bc[:, c:c + 1]
    dmat = jnp.where(row >= col, bcum + zt[c:c + 1, :], -jnp.inf)
    m_t = jnp.maximum(bcum + m_prev, jnp.max(dmat, axis=-1, keepdims=True))
    m_new = m_t[CHUNK - 1:CHUNK, :]
    b_last = bcum[CHUNK - 1:CHUNK, :]
    q = q_s[r0:r0 + CHUNK, :]
    k = k_s[r0:r0 + CHUNK, :]
    vaug = jnp.where(lane_w == ML_DH, 1.0, v_ref[r0:r0 + CHUNK, lanes]).astype(bf16)
    w_inter = jnp.exp(bcum + m_prev - m_t)
    w_intra = jnp.exp(dmat - m_t) * _dot_nt(q, k)
    num = w_inter * _dot(q, cmat.astype(bf16)) + _dot(w_intra.astype(bf16), vaug)
    decay = jnp.exp(b_last + m_prev - m_new)
    ws = jnp.exp(b_last - bcum + ib - m_new)
    cmat = decay * cmat + _dot_tn((ws * k.astype(f32)).astype(bf16), vaug)
    den = jnp.sum(jnp.where(lane_w == ML_DH, num, 0.0), axis=-1, keepdims=True)
    h = num / jnp.maximum(jnp.abs(den), jnp.exp(-m_t))
    hm = jnp.where(lane_w < ML_DH, h, 0.0)
    ms = jnp.sum(hm * hm, axis=-1, keepdims=True) * (1.0 / ML_DH)
    hn = hm * lax.rsqrt(ms + EPS) * ng
    y_ref[r0:r0 + CHUNK, lanes] = (hn * jax.nn.sigmoid(o_ref[r0:r0 + CHUNK, lanes])).astype(y_ref.dtype)
    return cmat, m_new


def _mixers_kernel(*refs):
    hg_in = refs[:4 * HG_PER]
    lb_ref, hng_ref, cx_ref, v_ref, o_ref, gt_ref, gb_ref, cw_ref, cb_ref, wqk_ref, mng_ref = refs[4 * HG_PER:4 * HG_PER + 11]
    yb_ref, yc_ref = refs[4 * HG_PER + 11:4 * HG_PER + 13]
    st_ref, hk_s, cum_s, c_ref, m_ref, carry_ref, buf_ref, q_s, k_s = refs[4 * HG_PER + 13:]
    tb = gt_ref.shape[0]
    group = pl.program_id(1)

    @pl.when(pl.program_id(2) == 0)
    def _():
        st_ref[...] = jnp.zeros_like(st_ref)
        c_ref[...] = jnp.zeros_like(c_ref)
        m_ref[...] = jnp.zeros_like(m_ref)
        carry_ref[...] = jnp.zeros_like(carry_ref)

    row = lax.broadcasted_iota(jnp.int32, (CHUNK, CHUNK), 0)
    col = lax.broadcasted_iota(jnp.int32, (CHUNK, CHUNK), 1)
    tril = (row >= col).astype(bf16)
    consts = _hg_consts()
    gb = gb_ref[...]
    hg_lanes = [slice(h * HG_D, (h + 1) * HG_D) for h in range(HG_PER)]
    ml_lanes = [slice(h * ML_DP, (h + 1) * ML_DP) for h in range(ML_PER)]

    for h in range(HG_PER):
        _hg_prelude(hg_in[4 * h + 1], lb_ref[:, hg_lanes[h]], tril, hk_s.at[h], cum_s.at[h])
    gates = []
    for h in range(ML_PER):
        gates.append(_ml_prelude(cx_ref, ml_lanes[h], gt_ref, gb, group * ML_PER + h, cw_ref[:, ml_lanes[h]],
                                 cb_ref[:, ml_lanes[h]], wqk_ref[h], tril, carry_ref.at[h], buf_ref.at[h],
                                 q_s.at[h], k_s.at[h]))

    sts = [st_ref[h] for h in range(HG_PER)]
    cms = [c_ref[h] for h in range(ML_PER)]
    mps = [m_ref[h][0:1, 0:1] for h in range(ML_PER)]
    for c in range(tb // CHUNK):
        for h in range(max(HG_PER, ML_PER)):
            if h < HG_PER:
                sts[h] = _hg_chunk(c, hg_in[4 * h], hg_in[4 * h + 2], hg_in[4 * h + 3], hng_ref[:, hg_lanes[h]], yb_ref,
                                   hg_lanes[h], hk_s.at[h], cum_s.at[h], sts[h], consts)
            if h < ML_PER:
                cms[h], mps[h] = _ml_chunk(c, gates[h], v_ref, o_ref, ml_lanes[h], mng_ref[:, ml_lanes[h]], yc_ref,
                                           q_s.at[h], k_s.at[h], cms[h], mps[h])
    for h in range(HG_PER):
        st_ref[h] = sts[h]
    for h in range(ML_PER):
        c_ref[h] = cms[h]
        m_ref[h] = jnp.broadcast_to(mps[h], m_ref.shape[1:])


def _mixers(proj, lb, hng, gb, cw, cb, wqk, mng, nb, seq, tb):
    t = proj.shape[0]
    nj = seq // tb
    hg_w, ml_w = HG_PER * HG_D, ML_PER * ML_DP

    def hseg(off, h):
        return pl.BlockSpec((tb, HG_D), lambda b, g, j, o=off // HG_D + h: (b * nj + j, o + HG_PER * g))

    def mseg(off):
        return pl.BlockSpec((tb, ml_w), lambda b, g, j, o=off // ml_w: (b * nj + j, o + g))

    in_specs = []
    for h in range(HG_PER):
        in_specs += [hseg(OFF_Q, h), hseg(OFF_F, h), hseg(OFF_I, h), hseg(OFF_OG, h)]
    in_specs += [
        pl.BlockSpec((1, hg_w), lambda b, g, j: (0, g)),
        pl.BlockSpec((1, hg_w), lambda b, g, j: (0, g)),
        mseg(OFF_CX), mseg(OFF_V), mseg(OFF_O),
        pl.BlockSpec((tb, 128), lambda b, g, j: (b * nj + j, OFF_IGFG // 128)),
        pl.BlockSpec((1, 128), lambda b, g, j: (0, 0)),
        pl.BlockSpec((ML_CONV, ml_w), lambda b, g, j: (0, g)),
        pl.BlockSpec((1, ml_w), lambda b, g, j: (0, g)),
        pl.BlockSpec((ML_PER, ML_DP, 2 * ML_DP), lambda b, g, j: (g, 0, 0)),
        pl.BlockSpec((1, ml_w), lambda b, g, j: (0, g)),
    ]
    return pl.pallas_call(
        _mixers_kernel,
        grid=(nb, HG_HEADS // HG_PER, nj),
        in_specs=in_specs,
        out_specs=[pl.BlockSpec((tb, hg_w), lambda b, g, j: (b * nj + j, g)),
                   pl.BlockSpec((tb, ml_w), lambda b, g, j: (b * nj + j, g))],
        out_shape=[jax.ShapeDtypeStruct((t, HG_WIDTH), bf16), jax.ShapeDtypeStruct((t, ML_WP), bf16)],
        scratch_shapes=[
            pltpu.VMEM((HG_PER, HG_D, HG_D), f32),
            pltpu.VMEM((HG_PER, tb, HG_D), f32),
            pltpu.VMEM((HG_PER, tb, HG_D), f32),
            pltpu.VMEM((ML_PER, ML_DP, ML_DP), f32),
            pltpu.VMEM((ML_PER, 8, 128), f32),
            pltpu.VMEM((ML_PER, 8, ML_DP), f32),
            pltpu.VMEM((ML_PER, tb + 8, ML_DP), f32),
            pltpu.VMEM((ML_PER, tb, ML_DP), bf16),
            pltpu.VMEM((ML_PER, tb, ML_DP), bf16),
        ],
        compiler_params=pltpu.CompilerParams(dimension_semantics=("parallel", "parallel", "arbitrary")),
        name="mixers",
    )(*([proj] * (4 * HG_PER)), lb, hng, proj, proj, proj, proj, gb, cw, cb, wqk, mng)


def _merge_kernel(g5_ref, yb_ref, yc_ref, sa_ref, sb_ref, sc_ref, wglu_ref, bglu_ref,
                  wa_ref, wb_ref, wc_ref, o_ref):
    g5 = g5_ref[...]
    ya = g5 * jax.nn.sigmoid(_dot(g5.astype(bf16), wglu_ref[...]) + bglu_ref[...])
    m = sa_ref[...].astype(f32) * _dot(ya.astype(bf16), wa_ref[...])
    m = m + sb_ref[...].astype(f32) * _dot(yb_ref[...], wb_ref[...])
    m = m + sc_ref[...].astype(f32) * _dot(yc_ref[...], wc_ref[...])
    o_ref[...] = m.astype(o_ref.dtype)


def _merge(g5, yb, yc, sg, wglu, bglu, wa, wb, wc, tm):
    t = g5.shape[0]
    d = D_MODEL

    def rows(w):
        return pl.BlockSpec((tm, w), lambda i: (i, 0))

    def full(a):
        return pl.BlockSpec(a.shape, lambda i: (0, 0))

    return pl.pallas_call(
        _merge_kernel,
        grid=(t // tm,),
        in_specs=[
            rows(S5_WIDTH), rows(HG_WIDTH), rows(ML_WP),
            pl.BlockSpec((tm, d), lambda i: (i, 0)),
            pl.BlockSpec((tm, d), lambda i: (i, 1)),
            pl.BlockSpec((tm, d), lambda i: (i, 2)),
            full(wglu), full(bglu), full(wa), full(wb), full(wc),
        ],
        out_specs=rows(d),
        out_shape=jax.ShapeDtypeStruct((t, d), bf16),
        compiler_params=pltpu.CompilerParams(dimension_semantics=("parallel",)),
        name="merge",
    )(g5, yb, yc, sg, sg, sg, wglu, bglu, wa, wb, wc)


def _proj_res_kernel(a_ref, w_ref, r_ref, o_ref):
    o_ref[...] = r_ref[...] + _dot(a_ref[...], w_ref[...])


def _proj_res(a, w, layer, res, tm):
    t, k = a.shape
    n = w.shape[2]
    return pl.pallas_call(
        _proj_res_kernel,
        grid=(t // tm,),
        in_specs=[
            pl.BlockSpec((tm, k), lambda i: (i, 0)),
            pl.BlockSpec((None, k, n), lambda i: (layer, 0, 0)),
            pl.BlockSpec((tm, n), lambda i: (i, 0)),
        ],
        out_specs=pl.BlockSpec((tm, n), lambda i: (i, 0)),
        out_shape=jax.ShapeDtypeStruct((t, n), f32),
        compiler_params=pltpu.CompilerParams(dimension_semantics=("parallel",)),
        name="proj_res",
    )(a, w, res)


def _ffn_kernel(x_ref, g_ref, wa_ref, wb_ref, cva_ref, cvb_ref, wd_ref, fg_ref, o_ref,
                hs_ref, bufa_ref, bufb_ref, cara_ref, carb_ref, *, tiles_per_seq, final):
    tm = x_ref.shape[0]
    i = pl.program_id(0)
    j = pl.program_id(1)

    @pl.when(j == 0)
    def _():
        x = x_ref[...]
        hs_ref[...] = _rms(x, g_ref[...]).astype(bf16)
        o_ref[...] = x

    @pl.when(i % tiles_per_seq == 0)
    def _():
        cara_ref[j] = jnp.zeros((8, FFN_TN), f32)
        carb_ref[j] = jnp.zeros((8, FFN_TN), f32)

    hs = hs_ref[...]

    def conv(w_ref, cv_ref, buf_ref, car_ref):
        u = _dot(hs, w_ref[...])
        buf_ref[0:8, :] = car_ref[j]
        buf_ref[8:, :] = u
        car_ref[j] = u[tm - 8:, :]
        y = cv_ref[FFN_CONV - 1:FFN_CONV, :] * u + cv_ref[FFN_CONV:FFN_CONV + 1, :]
        for kk in range(FFN_CONV - 1):
            sh = FFN_CONV - 1 - kk
            y = y + cv_ref[kk:kk + 1, :] * buf_ref[pl.ds(8 - sh, tm), :]
        return y

    a = conv(wa_ref, cva_ref, bufa_ref, cara_ref)
    b = conv(wb_ref, cvb_ref, bufb_ref, carb_ref)
    act = (jax.nn.silu(a) * b).astype(bf16)
    o_ref[...] += _dot(act, wd_ref[...])

    if final:
        @pl.when(j == FFN_NJ - 1)
        def _():
            o_ref[...] = _rms(o_ref[...], fg_ref[...])


def _ffn(x, g, wup, cv, wd, layer, fg, seq, tm, final):
    t, d = x.shape
    return pl.pallas_call(
        functools.partial(_ffn_kernel, tiles_per_seq=seq // tm, final=final),
        grid=(t // tm, FFN_NJ),
        in_specs=[
            pl.BlockSpec((tm, d), lambda i, j: (i, 0)),
            pl.BlockSpec((1, d), lambda i, j: (0, 0)),
            pl.BlockSpec((None, d, FFN_TN), lambda i, j: (layer, 0, j)),
            pl.BlockSpec((None, d, FFN_TN), lambda i, j: (layer, 0, j + FFN_NJ)),
            pl.BlockSpec((8, FFN_TN), lambda i, j: (0, j)),
            pl.BlockSpec((8, FFN_TN), lambda i, j: (0, j + FFN_NJ)),
            pl.BlockSpec((None, FFN_TN, d), lambda i, j: (layer, j, 0)),
            pl.BlockSpec((1, d), lambda i, j: (0, 0)),
        ],
        out_specs=pl.BlockSpec((tm, d), lambda i, j: (i, 0)),
        out_shape=jax.ShapeDtypeStruct((t, d), f32),
        scratch_shapes=[
            pltpu.VMEM((tm, d), bf16),
            pltpu.VMEM((tm + 8, FFN_TN), f32),
            pltpu.VMEM((tm + 8, FFN_TN), f32),
            pltpu.VMEM((FFN_NJ, 8, FFN_TN), f32),
            pltpu.VMEM((FFN_NJ, 8, FFN_TN), f32),
        ],
        compiler_params=pltpu.CompilerParams(dimension_semantics=("arbitrary", "arbitrary"),
                                             vmem_limit_bytes=VMEM_LIMIT),
        name="ffn",
    )(x, g, wup, wup, cv, cv, wd, fg)


def _pad_heads(w, axis):
    shp = w.shape
    w = w.reshape(shp[:axis] + (ML_HEADS, ML_DH) + shp[axis + 1:])
    pad = [(0, 0)] * w.ndim
    pad[axis + 1] = (0, ML_DP - ML_DH)
    w = jnp.pad(w, pad)
    return w.reshape(shp[:axis] + (ML_WP,) + shp[axis + 1:])


def _pack_w_in_kernel(w_ref, mix_ref, gate_ref):
    nproj = OFF_CX
    mix_ref[:, 0:nproj] = w_ref[:, 0:nproj].astype(bf16)
    for seg in range(3):
        for h in range(ML_HEADS):
            src = nproj + seg * ML_WIDTH + h * ML_DH
            dst = nproj + seg * ML_WP + h * ML_DP
            mix_ref[:, dst:dst + ML_DH] = w_ref[:, src:src + ML_DH].astype(bf16)
            mix_ref[:, dst + ML_DH:dst + ML_DP] = jnp.zeros((mix_ref.shape[0], ML_DP - ML_DH), bf16)
    src = nproj + 3 * ML_WIDTH
    mix_ref[:, OFF_IGFG:OFF_IGFG + 128] = jnp.zeros((mix_ref.shape[0], 128), bf16)
    mix_ref[:, OFF_IGFG:OFF_IGFG + 2 * ML_HEADS] = w_ref[:, src:src + 2 * ML_HEADS].astype(bf16)
    mix_ref[:, OFF_IGFG + 128:] = jnp.zeros((mix_ref.shape[0], MIXER_COLS - OFF_IGFG - 128), bf16)
    gate_ref[...] = w_ref[:, src + 2 * ML_HEADS:].astype(bf16)


def _pack_w_in(w, tr):
    nl, d, n = w.shape
    return pl.pallas_call(
        _pack_w_in_kernel,
        grid=(nl, d // tr),
        in_specs=[pl.BlockSpec((None, tr, n), lambda l, i: (l, i, 0))],
        out_specs=[pl.BlockSpec((None, tr, MIXER_COLS), lambda l, i: (l, i, 0)),
                   pl.BlockSpec((None, tr, GATE_COLS), lambda l, i: (l, i, 0))],
        out_shape=[jax.ShapeDtypeStruct((nl, d, MIXER_COLS), bf16), jax.ShapeDtypeStruct((nl, d, GATE_COLS), bf16)],
        compiler_params=pltpu.CompilerParams(dimension_semantics=("parallel", "parallel")),
        name="pack_w_in",
    )(w)


def _layer(x, p, big, layer, nb, seq, tiles, last):
    proj = _norm_proj(x, p["mix_norm"], big["w_mixer"], layer, act="none", out_dtype=f32,
                      tm=tiles["tm_proj"], tn=tiles["tn_mixer"])
    sg = _norm_proj(x, p["mix_norm"], big["w_gates"], layer, act="sigmoid", out_dtype=bf16,
                    tm=tiles["tm_proj"], tn=tiles["tn_gates"])
    g5 = _s5_core(proj, big["s5_m"], big["s5_bst"], big["s5_cst"], big["s5_av"], layer, nb)
    yb, yc = _mixers(proj, p["hg_lb"], p["hg_norm"], p["ml_gb"], p["ml_cw"], p["ml_cb"], p["ml_wqk"], p["ml_norm"],
                     nb, seq, tiles["tb"])
    merged = _merge(g5, yb, yc, sg, p["s5_wglu"], p["s5_bglu"], p["wb_a"], p["wb_b"], p["wb_c"], tiles["tm_merge"])
    x = _proj_res(merged, big["w_out"], layer, x, tiles["tm_merge"])
    return _ffn(x, p["ffn_norm"], big["ffn_wup"], p["ffn_cv"], big["ffn_wd"], layer, p["final_norm"], seq,
                tiles["tm_ffn"], last)


def _tiles(seq):
    return {
        "tm_proj": min(1024, seq), "tn_mixer": MIXER_COLS // 4, "tn_gates": GATE_COLS // 3,
        "tb": min(512, seq), "tm_merge": min(512, seq), "tm_ffn": min(1024, seq), "tr_pack": 256,
    }


def kernel(x, mix_norm, w_in, s5_lam_re, s5_lam_im, s5_log_dt, s5_b_re, s5_b_im, s5_c_re, s5_c_im, s5_d, s5_w_glu, s5_b_glu, hg_lower_bounds, hg_norm, ml_conv_w, ml_conv_b, ml_w_qk, ml_b_ig, ml_b_fg, ml_norm, w_branch, w_out, ffn_norm, ffn_w_up, ffn_conv_w, ffn_conv_b, ffn_w_down, final_norm):
    nb, seq, d = x.shape
    depth = w_in.shape[0]
    tiles = _tiles(seq)
    lbs = jax.nn.softmax(hg_lower_bounds.astype(f32), axis=0)
    lbs = jnp.cumsum(lbs, axis=0) - lbs[0:1]
    xs = x.astype(f32).reshape(nb * seq, d)
    w_mixer, w_gates = _pack_w_in(w_in.astype(bf16), tiles["tr_pack"])
    s5_m, s5_bst, s5_cst, s5_av = jax.vmap(_s5_prep)(
        s5_lam_re, s5_lam_im, s5_log_dt, s5_b_re, s5_b_im, s5_c_re, s5_c_im, s5_d)
    big = {"w_mixer": w_mixer, "w_gates": w_gates, "w_out": w_out.astype(bf16),
           "ffn_wup": ffn_w_up.astype(bf16), "ffn_wd": ffn_w_down.astype(bf16),
           "s5_m": s5_m, "s5_bst": s5_bst, "s5_cst": s5_cst, "s5_av": s5_av}
    for l in range(depth):
        wqk = ml_w_qk[l]
        wq = jnp.pad(wqk[:, :, :ML_DH], ((0, 0), (0, ML_DP - ML_DH), (0, ML_DP - ML_DH)))
        wk = jnp.pad(wqk[:, :, ML_DH:], ((0, 0), (0, ML_DP - ML_DH), (0, ML_DP - ML_DH)))
        gbias = jnp.concatenate([ml_b_ig[l], ml_b_fg[l], jnp.zeros((128 - 2 * ML_HEADS,), f32)])[None, :]
        wb = w_branch[l]
        ffn_cv = jnp.concatenate([ffn_conv_w[l], ffn_conv_b[l][None, :],
                                  jnp.zeros((8 - FFN_CONV - 1, 2 * FFN_DIM), f32)], axis=0)
        p = {
            "mix_norm": mix_norm[l][None, :],
            "s5_wglu": s5_w_glu[l].astype(bf16), "s5_bglu": s5_b_glu[l][None, :],
            "hg_lb": lbs[l][None, :], "hg_norm": hg_norm[l][None, :],
            "ml_gb": gbias, "ml_cw": _pad_heads(ml_conv_w[l], 1), "ml_cb": _pad_heads(ml_conv_b[l][None, :], 1),
            "ml_wqk": jnp.concatenate([wq, wk], axis=-1).astype(bf16),
            "ml_norm": _pad_heads(ml_norm[l][None, :], 1),
            "wb_a": wb[:S5_WIDTH].astype(bf16), "wb_b": wb[S5_WIDTH:S5_WIDTH + HG_WIDTH].astype(bf16),
            "wb_c": _pad_heads(wb[S5_WIDTH + HG_WIDTH:], 0).astype(bf16),
            "ffn_norm": ffn_norm[l][None, :], "ffn_cv": ffn_cv,
            "final_norm": final_norm[None, :],
        }
        xs = _layer(xs, p, big, l, nb, seq, tiles, l == depth - 1)
    return xs.reshape(nb, seq, d).astype(x.dtype)
```

```python
import functools

import jax
import jax.numpy as jnp
from jax import lax
from jax.experimental import pallas as pl
from jax.experimental.pallas import tpu as pltpu

f32 = jnp.float32
bf16 = jnp.bfloat16

D_MODEL = 2048
EPS = 1e-6
CHUNK = 64

S5_GROUPS = 32
S5_P = 16
S5_WIDTH = 512
S5_STATE = 64
S5_L = 8
S5_OCT = 8

HG_HEADS = 6
HG_D = 128
HG_WIDTH = 768
HG_SUB = 8
HG_PER = 3

ML_HEADS = 4
ML_DH = 192
ML_DP = 256
ML_WIDTH = 768
ML_WP = ML_HEADS * ML_DP
ML_CONV = 4
ML_PER = 2

FFN_DIM = 5632
FFN_CONV = 3
FFN_TN = 512
FFN_NJ = FFN_DIM // FFN_TN

OFF_U, OFF_Q, OFF_F, OFF_I, OFF_OG = 0, 512, 1280, 2048, 2816
OFF_CX, OFF_V, OFF_O, OFF_IGFG = 3584, 4608, 5632, 6656
MIXER_COLS = 7168
GATE_COLS = 3 * D_MODEL

VMEM_LIMIT = 60 * 1024 * 1024


def _split3(x):
    hi = x.astype(bf16)
    r1 = x - hi.astype(f32)
    mid = r1.astype(bf16)
    lo = (r1 - mid.astype(f32)).astype(bf16)
    return hi, mid, lo


def _dot(a, b):
    return jnp.dot(a, b, preferred_element_type=f32)


def _dot_nt(a, b):
    return lax.dot_general(a, b, (((1,), (1,)), ((), ())), preferred_element_type=f32)


def _dot_tn(a, b):
    return lax.dot_general(a, b, (((0,), (0,)), ((), ())), preferred_element_type=f32)


def _exact_dot(a_bf16, x):
    hi, mid, lo = _split3(x)
    return _dot(a_bf16, hi) + _dot(a_bf16, mid) + _dot(a_bf16, lo)


def _rms(x, g):
    ms = jnp.mean(x * x, axis=-1, keepdims=True)
    return x * lax.rsqrt(ms + EPS) * g


def _norm_proj_kernel(x_ref, g_ref, w_ref, o_ref, hs_ref, *, act):
    @pl.when(pl.program_id(1) == 0)
    def _():
        hs_ref[...] = _rms(x_ref[...], g_ref[...]).astype(bf16)

    y = _dot(hs_ref[...], w_ref[...])
    if act == "sigmoid":
        y = jax.nn.sigmoid(y)
    o_ref[...] = y.astype(o_ref.dtype)


def _norm_proj(x, g, w, layer, *, act, out_dtype, tm, tn):
    t, d = x.shape
    n = w.shape[2]
    return pl.pallas_call(
        functools.partial(_norm_proj_kernel, act=act),
        grid=(t // tm, n // tn),
        in_specs=[
            pl.BlockSpec((tm, d), lambda i, j: (i, 0)),
            pl.BlockSpec((1, d), lambda i, j: (0, 0)),
            pl.BlockSpec((None, d, tn), lambda i, j: (layer, 0, j)),
        ],
        out_specs=pl.BlockSpec((tm, tn), lambda i, j: (i, j)),
        out_shape=jax.ShapeDtypeStruct((t, n), out_dtype),
        scratch_shapes=[pltpu.VMEM((tm, d), bf16)],
        compiler_params=pltpu.CompilerParams(dimension_semantics=("parallel", "arbitrary"),
                                             vmem_limit_bytes=VMEM_LIMIT),
        name="norm_proj_" + act,
    )(x, g, w)


def _s5_kernel(u_ref, m_ref, bst_ref, cst_ref, av_ref, y_ref, sin_ref, xp_ref):
    nc = u_ref.shape[0] // S5_L
    half = S5_OCT * S5_STATE
    xb = jnp.concatenate([u_ref[pl.ds(t, nc, stride=S5_L), :] for t in range(S5_L)], axis=1).astype(bf16)
    sin_ref[...] = _dot(xb, bst_ref[0])
    ar = av_ref[0, 0:1, :]
    ai = av_ref[0, 1:2, :]

    def step(c, carry):
        xr, xi = carry
        xp_ref[pl.ds(c, 1), :] = jnp.concatenate([xr, xi], axis=1)
        s = sin_ref[pl.ds(c, 1), :]
        return ar * xr - ai * xi + s[:, :half], ar * xi + ai * xr + s[:, half:]

    z = jnp.zeros((1, half), f32)
    lax.fori_loop(0, nc, step, (z, z), unroll=8)
    y = jax.nn.gelu(_dot(xb, m_ref[0]) + _dot(xp_ref[...].astype(bf16), cst_ref[0]))
    for t in range(S5_L):
        y_ref[pl.ds(t, nc, stride=S5_L), :] = y[:, t * 128:(t + 1) * 128]


def _s5_core(proj, m, bst, cst, av, layer, nb):
    t = proj.shape[0]
    seq = t // nb
    nc = seq // S5_L
    noct = S5_GROUPS // S5_OCT
    cols = S5_L * 128
    st = 2 * S5_OCT * S5_STATE
    return pl.pallas_call(
        _s5_kernel,
        grid=(noct, nb),
        in_specs=[
            pl.BlockSpec((seq, 128), lambda o, b: (b, OFF_U // 128 + o)),
            pl.BlockSpec((None, 1, cols, cols), lambda o, b: (layer, o, 0, 0)),
            pl.BlockSpec((None, 1, cols, st), lambda o, b: (layer, o, 0, 0)),
            pl.BlockSpec((None, 1, st, cols), lambda o, b: (layer, o, 0, 0)),
            pl.BlockSpec((None, 1, 8, st // 2), lambda o, b: (layer, o, 0, 0)),
        ],
        out_specs=pl.BlockSpec((seq, 128), lambda o, b: (b, o)),
        out_shape=jax.ShapeDtypeStruct((t, S5_WIDTH), f32),
        scratch_shapes=[pltpu.VMEM((nc, st), f32), pltpu.VMEM((nc, st), f32)],
        compiler_params=pltpu.CompilerParams(dimension_semantics=("parallel", "parallel")),
        name="s5_core",
    )(proj, m, bst, cst, av)


def _s5_prep(lam_re, lam_im, log_dt, b_re, b_im, c_re, c_im, d_skip):
    hp = lax.Precision.HIGHEST
    g, n, p, ln, oc = S5_GROUPS, S5_STATE, S5_P, S5_L, S5_OCT
    no = g // oc
    dt = jnp.exp(log_dt)[:, None]
    lr, li = lam_re, lam_im
    mag = jnp.exp(lr * dt)
    ar, ai = mag * jnp.cos(li * dt), mag * jnp.sin(li * dt)
    den = lr * lr + li * li
    cr = ((ar - 1.0) * lr + ai * li) / den
    ci = (ai * lr - (ar - 1.0) * li) / den
    bbr = cr[..., None] * b_re - ci[..., None] * b_im
    bbi = cr[..., None] * b_im + ci[..., None] * b_re
    tau = jnp.arange(ln + 1, dtype=f32)[:, None, None]
    pmag = jnp.exp(tau * (lr * dt)[None])
    pr, pi = pmag * jnp.cos(tau * (li * dt)[None]), pmag * jnp.sin(tau * (li * dt)[None])
    cpr = c_re[None] * pr[:, :, None, :] - c_im[None] * pi[:, :, None, :]
    cpi = c_re[None] * pi[:, :, None, :] + c_im[None] * pr[:, :, None, :]
    kern = (jnp.einsum("tgpn,gnq->tgpq", cpr[:ln], bbr, precision=hp)
            - jnp.einsum("tgpn,gnq->tgpq", cpi[:ln], bbi, precision=hp))
    dskip = d_skip.reshape(g, p)
    kern = kern.at[0].add(jnp.eye(p, dtype=f32)[None] * dskip[:, :, None])
    kern = jnp.concatenate([kern, jnp.zeros((1, g, p, p), f32)], axis=0)
    eye = jnp.eye(oc, dtype=bf16)
    blk = kern.reshape(ln + 1, no, oc, p, p).transpose(0, 1, 2, 4, 3)
    blk = (blk.astype(bf16)[:, :, :, :, None, :] * eye[None, None, :, None, :, None]).reshape(ln + 1, no, oc * p, oc * p)
    s_idx = jnp.arange(ln)[:, None]
    t_idx = jnp.arange(ln)[None, :]
    lag = jnp.where(t_idx >= s_idx, t_idx - s_idx, ln)
    m8 = blk[lag].transpose(2, 0, 3, 1, 4).reshape(no, ln * oc * p, ln * oc * p)
    tau_b = (ln - 1) - tau[:ln]
    bmag = jnp.exp(tau_b * (lr * dt)[None])
    prs, pis = bmag * jnp.cos(tau_b * (li * dt)[None]), bmag * jnp.sin(tau_b * (li * dt)[None])
    bbrt, bbit = bbr.transpose(0, 2, 1), bbi.transpose(0, 2, 1)
    bst_r = prs[:, :, None, :] * bbrt[None] - pis[:, :, None, :] * bbit[None]
    bst_i = prs[:, :, None, :] * bbit[None] + pis[:, :, None, :] * bbrt[None]

    def expand_b(x):
        x = x.astype(bf16).reshape(ln, no, oc, p, n)
        return (x[:, :, :, :, None, :] * eye[None, None, :, None, :, None]).reshape(ln, no, oc * p, oc * n)

    bst8 = jnp.concatenate([expand_b(bst_r), expand_b(bst_i)], axis=-1)
    bst8 = bst8.transpose(1, 0, 2, 3).reshape(no, ln * oc * p, 2 * oc * n)
    cst = jnp.stack([cpr[1:], -cpi[1:]], axis=0)
    cst = cst.astype(bf16).reshape(2, ln, no, oc, p, n).transpose(0, 1, 2, 3, 5, 4)
    cst8 = cst[:, :, :, :, :, None, :] * eye[None, None, None, :, None, :, None]
    cst8 = cst8.reshape(2, ln, no, oc * n, oc * p).transpose(2, 0, 3, 1, 4).reshape(no, 2 * oc * n, ln * oc * p)
    av = jnp.stack([pr[ln].reshape(no, oc * n), pi[ln].reshape(no, oc * n)], axis=1)
    av = jnp.concatenate([av, jnp.zeros((no, 6, oc * n), f32)], axis=1)
    return m8, bst8, cst8, av


def _hg_consts():
    row = lax.broadcasted_iota(jnp.int32, (CHUNK, CHUNK), 0)
    col = lax.broadcasted_iota(jnp.int32, (CHUNK, CHUNK), 1)
    diag_mask = ((row // HG_SUB) == (col // HG_SUB)) & (col <= row)
    levels = []
    half = HG_SUB
    while half < CHUNK:
        levels.append((half, ((row // (2 * half)) == (col // (2 * half)))
                       & ((row & half) != 0) & ((col & half) == 0)))
        half *= 2
    wrow = lax.broadcasted_iota(jnp.int32, (HG_SUB * HG_D, CHUNK), 0)
    wcol = lax.broadcasted_iota(jnp.int32, (HG_SUB * HG_D, CHUNK), 1)
    wsel = ((wcol % HG_SUB) == (wrow // HG_D)).astype(bf16)
    return diag_mask, levels, wsel


def _hg_prelude(f_ref, lb, tril, k_s, cum_s):
    tb = f_ref.shape[0]
    f = lb + (1.0 - lb) * jax.nn.sigmoid(f_ref[...])
    k_s[...] = 1.0 - f
    logf = jnp.log2(f)
    for c in range(tb // CHUNK):
        cum_s[c * CHUNK:(c + 1) * CHUNK, :] = _exact_dot(tril, logf[c * CHUNK:(c + 1) * CHUNK])


def _hg_chunk(c, q_ref, i_ref, og_ref, ng, y_ref, lanes, k_s, cum_s, st, consts):
    diag_mask, levels, wsel = consts
    nblk = CHUNK // HG_SUB
    r0 = c * CHUNK
    q = jax.nn.silu(q_ref[r0:r0 + CHUNK, :])
    k = k_s[r0:r0 + CHUNK, :]
    cum = cum_s[r0:r0 + CHUNK, :]
    vb = i_ref[r0:r0 + CHUNK, :].astype(bf16)
    last = cum_s[r0 + CHUNK - 1:r0 + CHUNK, :]
    inter = _dot_nt((q * jnp.exp2(cum)).astype(bf16), st.astype(bf16))
    slabs = []
    for s in range(HG_SUB):
        pieces = []
        for b in range(nblk):
            rr = r0 + b * HG_SUB + s
            blk = slice(b * HG_SUB, (b + 1) * HG_SUB)
            dec = jnp.exp2(jnp.minimum(cum[blk] - cum_s[rr:rr + 1, :], 0.0))
            pieces.append((q[blk] * k_s[rr:rr + 1, :]) * dec)
        slabs.append(jnp.concatenate(pieces, axis=0).astype(bf16))
    sc = jnp.where(diag_mask, _dot(jnp.concatenate(slabs, axis=1), wsel), 0.0)
    for half, mask in levels:
        zs = []
        for base in range(0, CHUNK, 2 * half):
            mid = cum_s[r0 + base + half - 1:r0 + base + half, :]
            lo, hi = slice(base, base + half), slice(base + half, base + 2 * half)
            zs.append(k[lo] * jnp.exp2(mid - cum[lo]))
            zs.append(q[hi] * jnp.exp2(cum[hi] - mid))
        z = jnp.concatenate(zs, axis=0).astype(bf16)
        sc = jnp.where(mask, _dot_nt(z, z), sc)
    out = inter + _dot(sc.astype(bf16), vb)
    kdec = (k * jnp.exp2(last - cum)).astype(bf16)
    st = st * jnp.exp2(last) + _dot_tn(vb, kdec)
    o = _rms(out, ng) * jax.nn.silu(og_ref[r0:r0 + CHUNK, :])
    y_ref[r0:r0 + CHUNK, lanes] = o.astype(y_ref.dtype)
    return st


def _ml_prelude(cx_ref, lanes, gt_ref, gb, head, cw, cb, wqk, tril, carry_ref, buf_ref, q_s, k_s):
    tb = gt_ref.shape[0]
    nch = tb // CHUNK
    lane = lax.broadcasted_iota(jnp.int32, (CHUNK, 128), 1)
    lane_b = lax.broadcasted_iota(jnp.int32, (tb, 128), 1)
    gts = gt_ref[...] + gb
    ib_all = jnp.sum(jnp.where(lane_b == head, gts, 0.0), axis=-1, keepdims=True)
    lf_all = jax.nn.log_sigmoid(jnp.sum(jnp.where(lane_b == head + ML_HEADS, gts, 0.0), axis=-1, keepdims=True))
    lfc = jnp.zeros((CHUNK, 128), f32)
    ibc = jnp.zeros((CHUNK, 128), f32)
    for c in range(nch):
        lfc = jnp.where(lane == c, lf_all[c * CHUNK:(c + 1) * CHUNK], lfc)
        ibc = jnp.where(lane == c, ib_all[c * CHUNK:(c + 1) * CHUNK], ibc)
    bc = _exact_dot(tril, lfc)
    zt = (ibc - bc).T
    cx = cx_ref[:, lanes]
    buf_ref[0:8, :] = carry_ref[...]
    buf_ref[8:, :] = cx
    carry_ref[...] = cx[tb - 8:, :]
    ca = cw[ML_CONV - 1:ML_CONV, :] * cx + cb
    for kk in range(ML_CONV - 1):
        sh = ML_CONV - 1 - kk
        ca = ca + cw[kk:kk + 1, :] * buf_ref[pl.ds(8 - sh, tb), :]
    qk = _dot(jax.nn.silu(ca).astype(bf16), wqk)
    q_s[...] = qk[:, :ML_DP].astype(bf16)
    k_s[...] = (qk[:, ML_DP:] * (ML_DH ** -0.5)).astype(bf16)
    return ib_all, bc, zt


def _ml_chunk(c, gates, v_ref, o_ref, lanes, ng, y_ref, q_s, k_s, cmat, m_prev):
    ib_all, bc, zt = gates
    row = lax.broadcasted_iota(jnp.int32, (CHUNK, CHUNK), 0)
    col = lax.broadcasted_iota(jnp.int32, (CHUNK, CHUNK), 1)
    lane_w = lax.broadcasted_iota(jnp.int32, (CHUNK, ML_DP), 1)
    r0 = c * CHUNK
    ib = ib_all[r0:r0 + CHUNK]
    bcum = bc[:, c:c + 1]
    dmat = jnp.where(row >= col, bcum + zt[c:c + 1, :], -jnp.inf)
    m_t = jnp.maximum(bcum + m_prev, jnp.max(dmat, axis=-1, keepdims=True))
    m_new = m_t[CHUNK - 1:CHUNK, :]
    b_last = bcum[CHUNK - 1:CHUNK, :]
    q = q_s[r0:r0 + CHUNK, :]
    k = k_s[r0:r0 + CHUNK, :]
    vaug = jnp.where(lane_w == ML_DH, 1.0, v_ref[r0:r0 + CHUNK, lanes]).astype(bf16)
    w_inter = jnp.exp(bcum + m_prev - m_t)
    w_intra = jnp.exp(dmat - m_t) * _dot_nt(q, k)
    num = w_inter * _dot(q, cmat.astype(bf16)) + _dot(w_intra.astype(bf16), vaug)
    decay = jnp.exp(b_last + m_prev - m_new)
    ws = jnp.exp(b_last - bcum + ib - m_new)
    cmat = decay * cmat + _dot_tn((ws * k.astype(f32)).astype(bf16), vaug)
    den = jnp.sum(jnp.where(lane_w == ML_DH, num, 0.0), axis=-1, keepdims=True)
    h = num / jnp.maximum(jnp.abs(den), jnp.exp(-m_t))
    hm = jnp.where(lane_w < ML_DH, h, 0.0)
    ms = jnp.sum(hm * hm, axis=-1, keepdims=True) * (1.0 / ML_DH)
    hn = hm * lax.rsqrt(ms + EPS) * ng
    y_ref[r0:r0 + CHUNK, lanes] = (hn * jax.nn.sigmoid(o_ref[r0:r0 + CHUNK, lanes])).astype(y_ref.dtype)
    return cmat, m_new


def _mixers_kernel(*refs):
    hg_in = refs[:4 * HG_PER]
    lb_ref, hng_ref, cx_ref, v_ref, o_ref, gt_ref, gb_ref, cw_ref, cb_ref, wqk_ref, mng_ref = refs[4 * HG_PER:4 * HG_PER + 11]
    yb_ref, yc_ref = refs[4 * HG_PER + 11:4 * HG_PER + 13]
    st_ref, hk_s, cum_s, c_ref, m_ref, carry_ref, buf_ref, q_s, k_s = refs[4 * HG_PER + 13:]
    tb = gt_ref.shape[0]
    group = pl.program_id(1)

    @pl.when(pl.program_id(2) == 0)
    def _():
        st_ref[...] = jnp.zeros_like(st_ref)
        c_ref[...] = jnp.zeros_like(c_ref)
        m_ref[...] = jnp.zeros_like(m_ref)
        carry_ref[...] = jnp.zeros_like(carry_ref)

    row = lax.broadcasted_iota(jnp.int32, (CHUNK, CHUNK), 0)
    col = lax.broadcasted_iota(jnp.int32, (CHUNK, CHUNK), 1)
    tril = (row >= col).astype(bf16)
    consts = _hg_consts()
    gb = gb_ref[...]
    hg_lanes = [slice(h * HG_D, (h + 1) * HG_D) for h in range(HG_PER)]
    ml_lanes = [slice(h * ML_DP, (h + 1) * ML_DP) for h in range(ML_PER)]

    for h in range(HG_PER):
        _hg_prelude(hg_in[4 * h + 1], lb_ref[:, hg_lanes[h]], tril, hk_s.at[h], cum_s.at[h])
    gates = []
    for h in range(ML_PER):
        gates.append(_ml_prelude(cx_ref, ml_lanes[h], gt_ref, gb, group * ML_PER + h, cw_ref[:, ml_lanes[h]],
                                 cb_ref[:, ml_lanes[h]], wqk_ref[h], tril, carry_ref.at[h], buf_ref.at[h],
                                 q_s.at[h], k_s.at[h]))

    sts = [st_ref[h] for h in range(HG_PER)]
    cms = [c_ref[h] for h in range(ML_PER)]
    mps = [m_ref[h][0:1, 0:1] for h in range(ML_PER)]
    for c in range(tb // CHUNK):
        for h in range(max(HG_PER, ML_PER)):
            if h < HG_PER:
                sts[h] = _hg_chunk(c, hg_in[4 * h], hg_in[4 * h + 2], hg_in[4 * h + 3], hng_ref[:, hg_lanes[h]], yb_ref,
                                   hg_lanes[h], hk_s.at[h], cum_s.at[h], sts[h], consts)
            if h < ML_PER:
                cms[h], mps[h] = _ml_chunk(c, gates[h], v_ref, o_ref, ml_lanes[h], mng_ref[:, ml_lanes[h]], yc_ref,
                                           q_s.at[h], k_s.at[h], cms[h], mps[h])
    for h in range(HG_PER):
        st_ref[h] = sts[h]
    for h in range(ML_PER):
        c_ref[h] = cms[h]
        m_ref[h] = jnp.broadcast_to(mps[h], m_ref.shape[1:])


def _mixers(proj, lb, hng, gb, cw, cb, wqk, mng, nb, seq, tb):
    t = proj.shape[0]
    nj = seq // tb
    hg_w, ml_w = HG_PER * HG_D, ML_PER * ML_DP

    def hseg(off, h):
        return pl.BlockSpec((tb, HG_D), lambda b, g, j, o=off // HG_D + h: (b * nj + j, o + HG_PER * g))

    def mseg(off):
        return pl.BlockSpec((tb, ml_w), lambda b, g, j, o=off // ml_w: (b * nj + j, o + g))

    in_specs = []
    for h in range(HG_PER):
        in_specs += [hseg(OFF_Q, h), hseg(OFF_F, h), hseg(OFF_I, h), hseg(OFF_OG, h)]
    in_specs += [
        pl.BlockSpec((1, hg_w), lambda b, g, j: (0, g)),
        pl.BlockSpec((1, hg_w), lambda b, g, j: (0, g)),
        mseg(OFF_CX), mseg(OFF_V), mseg(OFF_O),
        pl.BlockSpec((tb, 128), lambda b, g, j: (b * nj + j, OFF_IGFG // 128)),
        pl.BlockSpec((1, 128), lambda b, g, j: (0, 0)),
        pl.BlockSpec((ML_CONV, ml_w), lambda b, g, j: (0, g)),
        pl.BlockSpec((1, ml_w), lambda b, g, j: (0, g)),
        pl.BlockSpec((ML_PER, ML_DP, 2 * ML_DP), lambda b, g, j: (g, 0, 0)),
        pl.BlockSpec((1, ml_w), lambda b, g, j: (0, g)),
    ]
    return pl.pallas_call(
        _mixers_kernel,
        grid=(nb, HG_HEADS // HG_PER, nj),
        in_specs=in_specs,
        out_specs=[pl.BlockSpec((tb, hg_w), lambda b, g, j: (b * nj + j, g)),
                   pl.BlockSpec((tb, ml_w), lambda b, g, j: (b * nj + j, g))],
        out_shape=[jax.ShapeDtypeStruct((t, HG_WIDTH), bf16), jax.ShapeDtypeStruct((t, ML_WP), bf16)],
        scratch_shapes=[
            pltpu.VMEM((HG_PER, HG_D, HG_D), f32),
            pltpu.VMEM((HG_PER, tb, HG_D), f32),
            pltpu.VMEM((HG_PER, tb, HG_D), f32),
            pltpu.VMEM((ML_PER, ML_DP, ML_DP), f32),
            pltpu.VMEM((ML_PER, 8, 128), f32),
            pltpu.VMEM((ML_PER, 8, ML_DP), f32),
            pltpu.VMEM((ML_PER, tb + 8, ML_DP), f32),
            pltpu.VMEM((ML_PER, tb, ML_DP), bf16),
            pltpu.VMEM((ML_PER, tb, ML_DP), bf16),
        ],
        compiler_params=pltpu.CompilerParams(dimension_semantics=("parallel", "parallel", "arbitrary")),
        name="mixers",
    )(*([proj] * (4 * HG_PER)), lb, hng, proj, proj, proj, proj, gb, cw, cb, wqk, mng)


def _merge_kernel(g5_ref, yb_ref, yc_ref, sa_ref, sb_ref, sc_ref, wglu_ref, bglu_ref,
                  wa_ref, wb_ref, wc_ref, wo_ref, x_ref, o_ref):
    g5 = g5_ref[...]
    ya = g5 * jax.nn.sigmoid(_dot(g5.astype(bf16), wglu_ref[...]) + bglu_ref[...])
    m = sa_ref[...].astype(f32) * _dot(ya.astype(bf16), wa_ref[...])
    m = m + sb_ref[...].astype(f32) * _dot(yb_ref[...], wb_ref[...])
    m = m + sc_ref[...].astype(f32) * _dot(yc_ref[...], wc_ref[...])
    o_ref[...] = x_ref[...] + _dot(m.astype(bf16), wo_ref[...])


def _merge(g5, yb, yc, sg, wglu, bglu, wa, wb, wc, wo, layer, x, tm):
    t = g5.shape[0]
    d = D_MODEL

    def rows(w):
        return pl.BlockSpec((tm, w), lambda i: (i, 0))

    def full(a):
        return pl.BlockSpec(a.shape, lambda i: (0, 0), pipeline_mode=pl.Buffered(1))

    return pl.pallas_call(
        _merge_kernel,
        grid=(t // tm,),
        in_specs=[
            rows(S5_WIDTH), rows(HG_WIDTH), rows(ML_WP),
            pl.BlockSpec((tm, d), lambda i: (i, 0)),
            pl.BlockSpec((tm, d), lambda i: (i, 1)),
            pl.BlockSpec((tm, d), lambda i: (i, 2)),
            full(wglu), full(bglu), full(wa), full(wb), full(wc),
            pl.BlockSpec((None, d, d), lambda i: (layer, 0, 0), pipeline_mode=pl.Buffered(1)),
            rows(d),
        ],
        out_specs=rows(d),
        out_shape=jax.ShapeDtypeStruct((t, d), f32),
        compiler_params=pltpu.CompilerParams(dimension_semantics=("parallel",), vmem_limit_bytes=VMEM_LIMIT),
        name="merge",
    )(g5, yb, yc, sg, sg, sg, wglu, bglu, wa, wb, wc, wo, x)


def _ffn_kernel(x_ref, g_ref, wa_ref, wb_ref, cva_ref, cvb_ref, wd_ref, fg_ref, o_ref,
                hs_ref, bufa_ref, bufb_ref, cara_ref, carb_ref, *, tiles_per_seq, final):
    tm = x_ref.shape[0]
    i = pl.program_id(0)
    j = pl.program_id(1)

    @pl.when(j == 0)
    def _():
        x = x_ref[...]
        hs_ref[...] = _rms(x, g_ref[...]).astype(bf16)
        o_ref[...] = x

    @pl.when(i % tiles_per_seq == 0)
    def _():
        cara_ref[j] = jnp.zeros((8, FFN_TN), f32)
        carb_ref[j] = jnp.zeros((8, FFN_TN), f32)

    hs = hs_ref[...]

    def conv(w_ref, cv_ref, buf_ref, car_ref):
        u = _dot(hs, w_ref[...])
        buf_ref[0:8, :] = car_ref[j]
        buf_ref[8:, :] = u
        car_ref[j] = u[tm - 8:, :]
        y = cv_ref[FFN_CONV - 1:FFN_CONV, :] * u + cv_ref[FFN_CONV:FFN_CONV + 1, :]
        for kk in range(FFN_CONV - 1):
            sh = FFN_CONV - 1 - kk
            y = y + cv_ref[kk:kk + 1, :] * buf_ref[pl.ds(8 - sh, tm), :]
        return y

    a = conv(wa_ref, cva_ref, bufa_ref, cara_ref)
    b = conv(wb_ref, cvb_ref, bufb_ref, carb_ref)
    act = (jax.nn.silu(a) * b).astype(bf16)
    o_ref[...] += _dot(act, wd_ref[...])

    if final:
        @pl.when(j == FFN_NJ - 1)
        def _():
            o_ref[...] = _rms(o_ref[...], fg_ref[...])


def _ffn(x, g, wup, cv, wd, layer, fg, seq, tm, final):
    t, d = x.shape
    return pl.pallas_call(
        functools.partial(_ffn_kernel, tiles_per_seq=seq // tm, final=final),
        grid=(t // tm, FFN_NJ),
        in_specs=[
            pl.BlockSpec((tm, d), lambda i, j: (i, 0)),
            pl.BlockSpec((1, d), lambda i, j: (0, 0)),
            pl.BlockSpec((None, d, FFN_TN), lambda i, j: (layer, 0, j)),
            pl.BlockSpec((None, d, FFN_TN), lambda i, j: (layer, 0, j + FFN_NJ)),
            pl.BlockSpec((8, FFN_TN), lambda i, j: (0, j)),
            pl.BlockSpec((8, FFN_TN), lambda i, j: (0, j + FFN_NJ)),
            pl.BlockSpec((None, FFN_TN, d), lambda i, j: (layer, j, 0)),
            pl.BlockSpec((1, d), lambda i, j: (0, 0)),
        ],
        out_specs=pl.BlockSpec((tm, d), lambda i, j: (i, 0)),
        out_shape=jax.ShapeDtypeStruct((t, d), f32),
        scratch_shapes=[
            pltpu.VMEM((tm, d), bf16),
            pltpu.VMEM((tm + 8, FFN_TN), f32),
            pltpu.VMEM((tm + 8, FFN_TN), f32),
            pltpu.VMEM((FFN_NJ, 8, FFN_TN), f32),
            pltpu.VMEM((FFN_NJ, 8, FFN_TN), f32),
        ],
        compiler_params=pltpu.CompilerParams(dimension_semantics=("arbitrary", "arbitrary"),
                                             vmem_limit_bytes=VMEM_LIMIT),
        name="ffn",
    )(x, g, wup, wup, cv, cv, wd, fg)


def _pad_heads(w, axis):
    shp = w.shape
    w = w.reshape(shp[:axis] + (ML_HEADS, ML_DH) + shp[axis + 1:])
    pad = [(0, 0)] * w.ndim
    pad[axis + 1] = (0, ML_DP - ML_DH)
    w = jnp.pad(w, pad)
    return w.reshape(shp[:axis] + (ML_WP,) + shp[axis + 1:])


def _pack_w_in_kernel(w_ref, mix_ref, gate_ref):
    nproj = OFF_CX
    mix_ref[:, 0:nproj] = w_ref[:, 0:nproj].astype(bf16)
    for seg in range(3):
        for h in range(ML_HEADS):
            src = nproj + seg * ML_WIDTH + h * ML_DH
            dst = nproj + seg * ML_WP + h * ML_DP
            mix_ref[:, dst:dst + ML_DH] = w_ref[:, src:src + ML_DH].astype(bf16)
            mix_ref[:, dst + ML_DH:dst + ML_DP] = jnp.zeros((mix_ref.shape[0], ML_DP - ML_DH), bf16)
    src = nproj + 3 * ML_WIDTH
    mix_ref[:, OFF_IGFG:OFF_IGFG + 128] = jnp.zeros((mix_ref.shape[0], 128), bf16)
    mix_ref[:, OFF_IGFG:OFF_IGFG + 2 * ML_HEADS] = w_ref[:, src:src + 2 * ML_HEADS].astype(bf16)
    mix_ref[:, OFF_IGFG + 128:] = jnp.zeros((mix_ref.shape[0], MIXER_COLS - OFF_IGFG - 128), bf16)
    gate_ref[...] = w_ref[:, src + 2 * ML_HEADS:].astype(bf16)


def _pack_w_in(w, tr):
    nl, d, n = w.shape
    return pl.pallas_call(
        _pack_w_in_kernel,
        grid=(nl, d // tr),
        in_specs=[pl.BlockSpec((None, tr, n), lambda l, i: (l, i, 0))],
        out_specs=[pl.BlockSpec((None, tr, MIXER_COLS), lambda l, i: (l, i, 0)),
                   pl.BlockSpec((None, tr, GATE_COLS), lambda l, i: (l, i, 0))],
        out_shape=[jax.ShapeDtypeStruct((nl, d, MIXER_COLS), bf16), jax.ShapeDtypeStruct((nl, d, GATE_COLS), bf16)],
        compiler_params=pltpu.CompilerParams(dimension_semantics=("parallel", "parallel")),
        name="pack_w_in",
    )(w)


def _layer(x, p, big, layer, nb, seq, tiles, last):
    proj = _norm_proj(x, p["mix_norm"], big["w_mixer"], layer, act="none", out_dtype=f32,
                      tm=tiles["tm_proj"], tn=tiles["tn_mixer"])
    sg = _norm_proj(x, p["mix_norm"], big["w_gates"], layer, act="sigmoid", out_dtype=bf16,
                    tm=tiles["tm_proj"], tn=tiles["tn_gates"])
    g5 = _s5_core(proj, big["s5_m"], big["s5_bst"], big["s5_cst"], big["s5_av"], layer, nb)
    yb, yc = _mixers(proj, p["hg_lb"], p["hg_norm"], p["ml_gb"], p["ml_cw"], p["ml_cb"], p["ml_wqk"], p["ml_norm"],
                     nb, seq, tiles["tb"])
    x = _merge(g5, yb, yc, sg, p["s5_wglu"], p["s5_bglu"], p["wb_a"], p["wb_b"], p["wb_c"], big["w_out"], layer, x,
               tiles["tm_merge"])
    return _ffn(x, p["ffn_norm"], big["ffn_wup"], p["ffn_cv"], big["ffn_wd"], layer, p["final_norm"], seq,
                tiles["tm_ffn"], last)


def _tiles(seq):
    return {
        "tm_proj": min(1024, seq), "tn_mixer": MIXER_COLS // 4, "tn_gates": GATE_COLS // 3,
        "tb": min(512, seq), "tm_merge": min(512, seq), "tm_ffn": min(1024, seq), "tr_pack": 256,
    }


def kernel(x, mix_norm, w_in, s5_lam_re, s5_lam_im, s5_log_dt, s5_b_re, s5_b_im, s5_c_re, s5_c_im, s5_d, s5_w_glu, s5_b_glu, hg_lower_bounds, hg_norm, ml_conv_w, ml_conv_b, ml_w_qk, ml_b_ig, ml_b_fg, ml_norm, w_branch, w_out, ffn_norm, ffn_w_up, ffn_conv_w, ffn_conv_b, ffn_w_down, final_norm):
    nb, seq, d = x.shape
    depth = w_in.shape[0]
    tiles = _tiles(seq)
    lbs = jax.nn.softmax(hg_lower_bounds.astype(f32), axis=0)
    lbs = jnp.cumsum(lbs, axis=0) - lbs[0:1]
    xs = x.astype(f32).reshape(nb * seq, d)
    w_mixer, w_gates = _pack_w_in(w_in.astype(bf16), tiles["tr_pack"])
    s5_m, s5_bst, s5_cst, s5_av = jax.vmap(_s5_prep)(
        s5_lam_re, s5_lam_im, s5_log_dt, s5_b_re, s5_b_im, s5_c_re, s5_c_im, s5_d)
    big = {"w_mixer": w_mixer, "w_gates": w_gates, "w_out": w_out.astype(bf16),
           "ffn_wup": ffn_w_up.astype(bf16), "ffn_wd": ffn_w_down.astype(bf16),
           "s5_m": s5_m, "s5_bst": s5_bst, "s5_cst": s5_cst, "s5_av": s5_av}
    for l in range(depth):
        wqk = ml_w_qk[l]
        wq = jnp.pad(wqk[:, :, :ML_DH], ((0, 0), (0, ML_DP - ML_DH), (0, ML_DP - ML_DH)))
        wk = jnp.pad(wqk[:, :, ML_DH:], ((0, 0), (0, ML_DP - ML_DH), (0, ML_DP - ML_DH)))
        gbias = jnp.concatenate([ml_b_ig[l], ml_b_fg[l], jnp.zeros((128 - 2 * ML_HEADS,), f32)])[None, :]
        wb = w_branch[l]
        ffn_cv = jnp.concatenate([ffn_conv_w[l], ffn_conv_b[l][None, :],
                                  jnp.zeros((8 - FFN_CONV - 1, 2 * FFN_DIM), f32)], axis=0)
        p = {
            "mix_norm": mix_norm[l][None, :],
            "s5_wglu": s5_w_glu[l].astype(bf16), "s5_bglu": s5_b_glu[l][None, :],
            "hg_lb": lbs[l][None, :], "hg_norm": hg_norm[l][None, :],
            "ml_gb": gbias, "ml_cw": _pad_heads(ml_conv_w[l], 1), "ml_cb": _pad_heads(ml_conv_b[l][None, :], 1),
            "ml_wqk": jnp.concatenate([wq, wk], axis=-1).astype(bf16),
            "ml_norm": _pad_heads(ml_norm[l][None, :], 1),
            "wb_a": wb[:S5_WIDTH].astype(bf16), "wb_b": wb[S5_WIDTH:S5_WIDTH + HG_WIDTH].astype(bf16),
            "wb_c": _pad_heads(wb[S5_WIDTH + HG_WIDTH:], 0).astype(bf16),
            "ffn_norm": ffn_norm[l][None, :], "ffn_cv": ffn_cv,
            "final_norm": final_norm[None, :],
        }
        xs = _layer(xs, p, big, l, nb, seq, tiles, l == depth - 1)
    return xs.reshape(nb, seq, d).astype(x.dtype)
```
